```python
import math
import jax, jax.numpy as jnp
from jax import lax
import numpy as np

D_MODEL = 2048
BATCH = 4
SEQ = 2048
DEPTH = 4

MEM_LEN = 256
EPS = 1e-6
POOL_DIM = D_MODEL // 2
POOL_WINDOWS = (2, 4, 8, 16)
POOL_GROUPS = len(POOL_WINDOWS)
POOL_GROUP_DIM = POOL_DIM // POOL_GROUPS
HGRN_DIM = D_MODEL // 2
HGRN_EXPAND = 128
HGRN_HEADS = HGRN_DIM // HGRN_EXPAND
HGRN_KEY_DIM = HGRN_EXPAND
HGRN_VAL_DIM = HGRN_DIM // HGRN_HEADS
FORGET_DIM = HGRN_HEADS * HGRN_KEY_DIM
CHUNK = 64
X_HEADS = 4
X_HEAD_DIM = D_MODEL // X_HEADS
D_FF = -(-(8 * D_MODEL) // (3 * 256)) * 256
IN_COLS = POOL_DIM + 2 * FORGET_DIM + 2 * HGRN_DIM + 2 * D_MODEL
IN_SPLITS = (
    POOL_DIM,
    POOL_DIM + FORGET_DIM,
    POOL_DIM + 2 * FORGET_DIM,
    POOL_DIM + 2 * FORGET_DIM + HGRN_DIM,
    POOL_DIM + 2 * FORGET_DIM + 2 * HGRN_DIM,
    POOL_DIM + 2 * FORGET_DIM + 2 * HGRN_DIM + D_MODEL,
)

kernel_name = "hybrid_pool_hgrn2_gated_xattn_swiglu"


def rmsnorm(x, gain):
    x32 = x.astype(jnp.float32)
    y = x32 * lax.rsqrt(jnp.mean(x32 * x32, axis=-1, keepdims=True) + EPS)
    return (y * gain.astype(jnp.float32)).astype(x.dtype)


def causal_multiscale_pool(a, w_group, scale):
    B_, S_, _ = a.shape
    a32 = a.astype(jnp.float32)
    csum = jnp.pad(jnp.cumsum(a32, axis=1), ((0, 0), (1, 0), (0, 0)))
    t = np.arange(S_)
    outs = []
    for g, w in enumerate(POOL_WINDOWS):
        lo, hi = g * POOL_GROUP_DIM, (g + 1) * POOL_GROUP_DIM
        c = csum[..., lo:hi]
        start = np.maximum(t + 1 - w, 0)
        count = jnp.asarray(np.minimum(t + 1, w).astype(np.float32))[None, :, None]
        window_mean = (c[:, 1:] - c[:, start]) / count
        outs.append(window_mean - a32[..., lo:hi])
    pooled = jnp.stack(outs, axis=2).astype(a.dtype)
    mixed = jnp.einsum('bsgc,gcd->bsgd', pooled, w_group)
    return mixed.reshape(B_, S_, POOL_DIM) * scale


def hgrn2_chunked(q, k, v, log_f):
    B_, S_, H, K = q.shape
    V = v.shape[-1]
    nc = S_ // CHUNK

    def to_chunks(z):
        return z.reshape(B_, nc, CHUNK, H, z.shape[-1]).transpose(1, 0, 3, 2, 4)

    qc, kc, vc, gc = to_chunks(q), to_chunks(k), to_chunks(v), to_chunks(log_f)
    causal = jnp.tril(jnp.ones((CHUNK, CHUNK), dtype=bool))

    def step(state, inp):
        q_, k_, v_, g_ = inp
        b = jnp.cumsum(g_, axis=2)
        rel = b[:, :, :, None, :] - b[:, :, None, :, :]
        decay = jnp.exp(jnp.where(causal[:, :, None], rel, -jnp.inf))
        scores = jnp.einsum('bhtsk,bhsk->bhts', decay * q_[:, :, :, None, :], k_)
        o_intra = jnp.einsum('bhts,bhsv->bhtv', scores, v_)
        o_inter = jnp.einsum('bhtk,bhkv->bhtv', q_ * jnp.exp(b), state)
        b_last = b[:, :, -1:, :]
        k_dec = k_ * jnp.exp(b_last - b)
        new_state = state * jnp.exp(b_last[:, :, 0, :])[..., None] \
            + jnp.einsum('bhsk,bhsv->bhkv', k_dec, v_)
        return new_state, o_intra + o_inter

    state0 = jnp.zeros((B_, H, K, V), jnp.float32)
    _, o = lax.scan(step, state0, (qc, kc, vc, gc))
    return o.transpose(1, 0, 3, 2, 4).reshape(B_, S_, H, V)


def hgrn2_branch(z_q, z_f, z_i, z_og, lb, norm_gain):
    B_, S_, _ = z_q.shape
    dt = z_i.dtype
    zf = z_f.astype(jnp.float32)
    lb = lb.astype(jnp.float32)
    q = jax.nn.silu(z_q.astype(jnp.float32))
    log_f = jnp.logaddexp(jnp.log(lb), jnp.log1p(-lb) + jax.nn.log_sigmoid(zf))
    k = (1.0 - lb) * jax.nn.sigmoid(-zf)
    shp_k = (B_, S_, HGRN_HEADS, HGRN_KEY_DIM)
    o = hgrn2_chunked(q.reshape(shp_k), k.reshape(shp_k),
                      z_i.astype(jnp.float32).reshape(B_, S_, HGRN_HEADS, HGRN_VAL_DIM),
                      log_f.reshape(shp_k))
    o = o * lax.rsqrt(jnp.mean(o * o, axis=-1, keepdims=True) + EPS)
    o = o.reshape(B_, S_, HGRN_DIM) * norm_gain.astype(jnp.float32)
    return (o * jax.nn.silu(z_og.astype(jnp.float32))).astype(dt)


def memory_cross_attention(h, mem_n, w_q, w_kv, w_o):
    B_, S_, _ = h.shape
    M_ = mem_n.shape[1]
    q = (h @ w_q).reshape(B_, S_, X_HEADS, X_HEAD_DIM)
    k, v = jnp.split(mem_n @ w_kv, 2, axis=-1)
    k = k.reshape(B_, M_, X_HEADS, X_HEAD_DIM)
    v = v.reshape(B_, M_, X_HEADS, X_HEAD_DIM)
    s = jnp.einsum('bshd,bmhd->bhsm', q, k).astype(jnp.float32) * (X_HEAD_DIM ** -0.5)
    p = jax.nn.softmax(s, axis=-1).astype(h.dtype)
    o = jnp.einsum('bhsm,bmhd->bshd', p, v).reshape(B_, S_, D_MODEL)
    return o @ w_o


def swiglu(h, w_in, w_out):
    g, u = jnp.split(h @ w_in, 2, axis=-1)
    return (jax.nn.silu(g) * u) @ w_out


def setup_inputs(seed: int = 0) -> dict:
    key = jax.random.key(seed)
    ks = jax.random.split(key, 24)
    f32 = jnp.float32

    def nrm(k, shape, fan_in):
        return jax.random.normal(k, shape, f32) * (fan_in ** -0.5)

    def gain(k, shape):
        return 1.0 + 0.02 * jax.random.normal(k, shape, f32)

    return {
        "x": jax.random.normal(ks[0], (BATCH, SEQ, D_MODEL), f32),
        "mem": jax.random.normal(ks[1], (BATCH, MEM_LEN, D_MODEL), f32),
        "w_in": nrm(ks[2], (DEPTH, D_MODEL, IN_COLS), D_MODEL),
        "w_pool_group": nrm(ks[3], (DEPTH, POOL_GROUPS, POOL_GROUP_DIM, POOL_GROUP_DIM), POOL_GROUP_DIM),
        "pool_scale": gain(ks[4], (DEPTH, POOL_DIM)),
        "hgrn_lower_bounds": 0.1 * jax.random.normal(ks[5], (DEPTH, FORGET_DIM), f32),
        "hgrn_norm": gain(ks[6], (DEPTH, HGRN_DIM)),
        "w_branch_pool": nrm(ks[7], (DEPTH, POOL_DIM, D_MODEL), POOL_DIM),
        "w_branch_hgrn": nrm(ks[8], (DEPTH, HGRN_DIM, D_MODEL), HGRN_DIM),
        "w_mix_out": nrm(ks[9], (DEPTH, D_MODEL, D_MODEL), D_MODEL),
        "norm_mix": gain(ks[10], (DEPTH, D_MODEL)),
        "norm_mem": gain(ks[11], (DEPTH, D_MODEL)),
        "norm_cross": gain(ks[12], (DEPTH, D_MODEL)),
        "w_xq": nrm(ks[13], (DEPTH, D_MODEL, D_MODEL), D_MODEL),
        "w_xkv": nrm(ks[14], (DEPTH, D_MODEL, 2 * D_MODEL), D_MODEL),
        "w_xo": nrm(ks[15], (DEPTH, D_MODEL, D_MODEL), D_MODEL),
        "norm_ffn": gain(ks[16], (DEPTH, D_MODEL)),
        "w_ffn_in": nrm(ks[17], (DEPTH, D_MODEL, 2 * D_FF), D_MODEL),
        "w_ffn_out": nrm(ks[18], (DEPTH, D_FF, D_MODEL), D_FF),
        "norm_final": gain(ks[19], (D_MODEL,)),
    }


def reference(x, mem, w_in, w_pool_group, pool_scale, hgrn_lower_bounds, hgrn_norm,
              w_branch_pool, w_branch_hgrn, w_mix_out, norm_mix, norm_mem, norm_cross,
              w_xq, w_xkv, w_xo, norm_ffn, w_ffn_in, w_ffn_out, norm_final):
    lb_all = jnp.cumsum(jax.nn.softmax(hgrn_lower_bounds.astype(jnp.float32), axis=0), axis=0)
    lb_all = lb_all - lb_all[0:1]
    h = x
    for l in range(DEPTH):
        u = rmsnorm(h, norm_mix[l])
        z = u @ w_in[l]
        z_pool, z_q, z_f, z_i, z_og, z_ga, z_gb = jnp.split(z, IN_SPLITS, axis=-1)
        a_out = causal_multiscale_pool(z_pool, w_pool_group[l], pool_scale[l])
        b_out = hgrn2_branch(z_q, z_f, z_i, z_og, lb_all[l], hgrn_norm[l])
        merged = jax.nn.sigmoid(z_ga) * (a_out @ w_branch_pool[l]) \
            + jax.nn.sigmoid(z_gb) * (b_out @ w_branch_hgrn[l])
        h = h + merged @ w_mix_out[l]
        mem_n = rmsnorm(mem, norm_mem[l])
        h = h + memory_cross_attention(rmsnorm(h, norm_cross[l]), mem_n,
                                       w_xq[l], w_xkv[l], w_xo[l])
        h = h + swiglu(rmsnorm(h, norm_ffn[l]), w_ffn_in[l], w_ffn_out[l])
    return rmsnorm(h, norm_final)
```

```python
import functools

import numpy as np
import jax
import jax.numpy as jnp
from jax import lax
from jax.experimental import pallas as pl
from jax.experimental.pallas import tpu as pltpu

F32 = jnp.float32
BF16 = jnp.bfloat16
EPS = 1e-6

POOL_WINDOWS = (2, 4, 8, 16)
POOL_HALO = 16
HGRN_HEAD = 128
HGRN_CHUNK = 128
X_HEADS = 4

VMEM_LIMIT = 56 * 1024 * 1024


def _params(sem):
    return pltpu.CompilerParams(dimension_semantics=sem, vmem_limit_bytes=VMEM_LIMIT)


def _resident(block_shape, index_map):
    return pl.BlockSpec(block_shape, index_map, pipeline_mode=pl.Buffered(1))


def _rmsnorm(x, gain):
    ms = jnp.mean(x * x, axis=-1, keepdims=True)
    return x * lax.rsqrt(ms + EPS) * gain


def _dot(a, b):
    return jnp.dot(a, b, preferred_element_type=F32)


def _dot_nt(a, b):
    return lax.dot_general(a, b, (((1,), (1,)), ((), ())), preferred_element_type=F32)


def _dot_tn(a, b):
    return lax.dot_general(a, b, (((0,), (0,)), ((), ())), preferred_element_type=F32)


def _sigmoid(x):
    return 1.0 / (1.0 + jnp.exp(-x))


def _norm_mm_kernel(x_ref, g_ref, w_ref, o_ref, u_ref):
    @pl.when(pl.program_id(1) == 0)
    def _():
        u_ref[...] = _rmsnorm(x_ref[...], g_ref[...]).astype(BF16)

    o_ref[...] = _dot(u_ref[...], w_ref[...]).astype(o_ref.dtype)


def _norm_mm(x, gain_all, w_all, layer, out_dtype, tm, tn):
    m, d = x.shape
    n = w_all.shape[-1]
    return pl.pallas_call(
        _norm_mm_kernel,
        grid=(m // tm, n // tn),
        in_specs=[
            pl.BlockSpec((tm, d), lambda i, j: (i, 0)),
            pl.BlockSpec((None, 1, d), lambda i, j: (layer, 0, 0)),
            pl.BlockSpec((None, d, tn), lambda i, j: (layer, 0, j)),
        ],
        out_specs=pl.BlockSpec((tm, tn), lambda i, j: (i, j)),
        out_shape=jax.ShapeDtypeStruct((m, n), out_dtype),
        scratch_shapes=[pltpu.VMEM((tm, d), BF16)],
        compiler_params=_params(("parallel", "arbitrary")),
        name="norm_mm",
    )(x, gain_all, w_all)


def _hgrn_tables(c):
    nlev = int(np.log2(c))
    t = np.arange(c)[:, None]
    j = np.arange(c)[None, :]
    blocks = [(j <= t)]
    for l in range(1, nlev + 1):
        m = 2 ** (l - 1)
        r = (t // (2 * m)) * (2 * m) + m - 1
        right = (t % (2 * m)) >= m
        blocks.append(np.where(right, (j > r) & (j <= t), (j > t) & (j <= r)))
    blocks.append(j > t)
    mall = np.concatenate(blocks, axis=0).astype(np.float32)
    x = t ^ j
    lev = np.zeros((c, c), np.int32)
    low = j < t
    lev[low] = np.floor(np.log2(x[low])).astype(np.int32) + 1
    lev[np.arange(c), np.arange(c)] = nlev + 1
    return mall, lev, nlev


def _hgrn_kernel(lb_ref, q_ref, f_ref, i_ref, og_ref, gain_ref, mall_ref, lev_ref,
                 o_ref, state_ref, lbs_ref, *, layer, nlev, heads):
    c = HGRN_CHUNK

    @pl.when(pl.program_id(1) == 0)
    def _():
        state_ref[...] = jnp.zeros_like(state_ref)
        x = lb_ref[...]
        e = jnp.exp(x - jnp.max(x, axis=0, keepdims=True))
        sm = e / jnp.sum(e, axis=0, keepdims=True)
        lb = jnp.zeros_like(sm[0:1])
        for r in range(1, layer + 1):
            lb = lb + sm[r:r + 1]
        lbs_ref[0:1, :] = jnp.log(lb)
        lbs_ref[1:2, :] = jnp.log1p(-lb)
        lbs_ref[2:3, :] = 1.0 - lb

    lev = lev_ref[...]
    mall = mall_ref[...]

    def head(h, carry):
        hs = pl.ds(pl.multiple_of(h * HGRN_HEAD, HGRN_HEAD), HGRN_HEAD)
        zq = q_ref[:, hs].astype(F32)
        q = zq * _sigmoid(zq)
        zf = f_ref[:, hs]
        log_lb = lbs_ref[0:1, hs]
        log_1m_lb = lbs_ref[1:2, hs]
        one_m_lb = lbs_ref[2:3, hs]
        e = jnp.exp(-jnp.abs(zf))
        rcp = 1.0 / (1.0 + e)
        k = one_m_lb * jnp.where(zf >= 0, e * rcp, rcp)
        cc = log_1m_lb + (jnp.minimum(zf, 0.0) - jnp.log1p(e))
        g = jnp.maximum(log_lb, cc) + jnp.log1p(jnp.exp(-jnp.abs(log_lb - cc)))
        g_hi = g.astype(BF16)
        g_lo = (g - g_hi.astype(F32)).astype(BF16)
        e2 = _dot(mall, jnp.concatenate([g_hi, g_lo], axis=1))
        w = jnp.exp(e2[:, :HGRN_HEAD] + e2[:, HGRN_HEAD:])

        s = jnp.zeros((c, c), F32)
        for l in range(1, nlev + 1):
            wl = w[l * c:(l + 1) * c]
            s_l = _dot_nt((q * wl).astype(BF16), (k * wl).astype(BF16))
            s = jnp.where(lev == l, s_l, s)
        s = jnp.where(lev == nlev + 1, jnp.sum(q * k, axis=1, keepdims=True), s)

        v = i_ref[:, hs]
        state_t = state_ref[h]
        o = _dot(s.astype(BF16), v)
        o = o + _dot_nt((q * w[0:c]).astype(BF16), state_t.astype(BF16))
        k_dec = (k * w[(nlev + 1) * c:(nlev + 2) * c]).astype(BF16)
        state_ref[h] = state_t * w[c - 1:c] + _dot_tn(v, k_dec)

        o = o * lax.rsqrt(jnp.mean(o * o, axis=1, keepdims=True) + EPS) * gain_ref[:, hs]
        zo = og_ref[:, hs].astype(F32)
        o_ref[:, hs] = (o * (zo * _sigmoid(zo))).astype(o_ref.dtype)
        return carry

    lax.fori_loop(0, heads, head, 0)


def _hgrn(zr, zf, lb_raw, gain_all, layer, batch, seq, mall, lev, nlev):
    m = zr.shape[0]
    hd = zf.shape[1]
    depth = lb_raw.shape[0]
    c = HGRN_CHUNK
    nc = seq // c
    heads = hd // HGRN_HEAD
    row = lambda b, t: b * nc + t
    return pl.pallas_call(
        functools.partial(_hgrn_kernel, layer=layer, nlev=nlev, heads=heads),
        grid=(batch, nc),
        in_specs=[
            _resident((depth, hd), lambda b, t: (0, 0)),
            pl.BlockSpec((c, hd), lambda b, t: (row(b, t), 1)),
            pl.BlockSpec((c, hd), lambda b, t: (row(b, t), 0)),
            pl.BlockSpec((c, hd), lambda b, t: (row(b, t), 2)),
            pl.BlockSpec((c, hd), lambda b, t: (row(b, t), 3)),
            pl.BlockSpec((None, 1, hd), lambda b, t: (layer, 0, 0)),
            _resident(mall.shape, lambda b, t: (0, 0)),
            _resident(lev.shape, lambda b, t: (0, 0)),
        ],
        out_specs=pl.BlockSpec((c, hd), lambda b, t: (row(b, t), 0)),
        out_shape=jax.ShapeDtypeStruct((m, hd), BF16),
        scratch_shapes=[pltpu.VMEM((heads, HGRN_HEAD, HGRN_HEAD), F32),
                        pltpu.VMEM((8, hd), F32)],
        compiler_params=_params(("parallel", "arbitrary")),
        name="hgrn",
    )(lb_raw, zr, zf, zr, zr, gain_all, mall, lev)


def _mix_kernel(zp_ref, halo_ref, ga_ref, gb_ref, bo_ref, h_ref, wg_ref, sc_ref,
                wbp_ref, wbh_ref, wmo_ref, o_ref, *, tm):
    t0 = pl.program_id(1) * tm
    x = zp_ref[...].astype(F32)
    halo = jnp.where(t0 > 0, halo_ref[...].astype(F32), 0.0)
    ext = jnp.concatenate([halo, x], axis=0)
    pos = lax.broadcasted_iota(jnp.int32, (tm, 1), 0) + t0
    gd = x.shape[1] // len(POOL_WINDOWS)
    outs = []
    for g, wdw in enumerate(POOL_WINDOWS):
        cols = slice(g * gd, (g + 1) * gd)
        e = ext[:, cols]
        sh = 1
        while sh < wdw:
            e = e + pltpu.roll(e, sh, axis=0)
            sh *= 2
        cnt = jnp.minimum(pos + 1, wdw).astype(F32)
        pooled = (e[POOL_HALO:] / cnt - x[:, cols]).astype(BF16)
        outs.append((_dot(pooled, wg_ref[g]) * sc_ref[:, cols]).astype(BF16))
    a_out = jnp.concatenate(outs, axis=1)
    merged = _sigmoid(ga_ref[...].astype(F32)) * _dot(a_out, wbp_ref[...])
    merged = merged + _sigmoid(gb_ref[...].astype(F32)) * _dot(bo_ref[...], wbh_ref[...])
    o_ref[...] = h_ref[...] + _dot(merged.astype(BF16), wmo_ref[...])


def _mix(zr, b_out, h, wg_all, sc_all, wbp_all, wbh_all, wmo_all, layer, batch, seq, tm):
    m, d = h.shape
    pd = b_out.shape[1]
    ng, gd = wg_all.shape[1], wg_all.shape[2]
    nt = seq // tm
    row = lambda b, t: b * nt + t
    hb = tm // POOL_HALO
    return pl.pallas_call(
        functools.partial(_mix_kernel, tm=tm),
        grid=(batch, nt),
        in_specs=[
            pl.BlockSpec((tm, pd), lambda b, t: (row(b, t), 0)),
            pl.BlockSpec((POOL_HALO, pd), lambda b, t: (jnp.maximum(row(b, t) * hb - 1, 0), 0)),
            pl.BlockSpec((tm, d), lambda b, t: (row(b, t), 2)),
            pl.BlockSpec((tm, d), lambda b, t: (row(b, t), 3)),
            pl.BlockSpec((tm, pd), lambda b, t: (row(b, t), 0)),
            pl.BlockSpec((tm, d), lambda b, t: (row(b, t), 0)),
            _resident((None, ng, gd, gd), lambda b, t: (layer, 0, 0, 0)),
            _resident((None, 1, pd), lambda b, t: (layer, 0, 0)),
            _resident((None, pd, d), lambda b, t: (layer, 0, 0)),
            _resident((None, pd, d), lambda b, t: (layer, 0, 0)),
            _resident((None, d, d), lambda b, t: (layer, 0, 0)),
        ],
        out_specs=pl.BlockSpec((tm, d), lambda b, t: (row(b, t), 0)),
        out_shape=jax.ShapeDtypeStruct((m, d), F32),
        compiler_params=_params(("parallel", "parallel")),
        name="mix",
    )(zr, zr, zr, zr, b_out, h, wg_all, sc_all, wbp_all, wbh_all, wmo_all)


def _xattn_kernel(h_ref, g_ref, wq_ref, kv_ref, wo_ref, o_ref, att_ref):
    h = h_ref[...]
    d = h.shape[1]
    dh = d // X_HEADS
    u = _rmsnorm(h, g_ref[...]).astype(BF16)
    q = _dot(u, wq_ref[...]).astype(BF16)
    for hh in range(X_HEADS):
        cols = slice(hh * dh, (hh + 1) * dh)
        s = _dot_nt(q[:, cols], kv_ref[:, cols]) * (dh ** -0.5)
        p = jnp.exp(s - jnp.max(s, axis=-1, keepdims=True))
        p = p / jnp.sum(p, axis=-1, keepdims=True)
        att_ref[:, cols] = _dot(p.astype(BF16), kv_ref[:, d + hh * dh:d + (hh + 1) * dh]).astype(BF16)
    o_ref[...] = h + _dot(att_ref[...], wo_ref[...])


def _xattn(h, gain_all, wq_all, kv, wo_all, layer, batch, seq, mem_len, tm):
    m, d = h.shape
    nt = seq // tm
    row = lambda b, t: b * nt + t
    return pl.pallas_call(
        _xattn_kernel,
        grid=(batch, nt),
        in_specs=[
            pl.BlockSpec((tm, d), lambda b, t: (row(b, t), 0)),
            pl.BlockSpec((None, 1, d), lambda b, t: (layer, 0, 0)),
            _resident((None, d, d), lambda b, t: (layer, 0, 0)),
            pl.BlockSpec((mem_len, 2 * d), lambda b, t: (b, 0)),
            _resident((None, d, d), lambda b, t: (layer, 0, 0)),
        ],
        out_specs=pl.BlockSpec((tm, d), lambda b, t: (row(b, t), 0)),
        out_shape=jax.ShapeDtypeStruct((m, d), F32),
        scratch_shapes=[pltpu.VMEM((tm, d), BF16)],
        compiler_params=_params(("parallel", "parallel")),
        name="xattn",
    )(h, gain_all, wq_all, kv, wo_all)


def _ffn_kernel(h_ref, g_ref, wg_ref, wu_ref, wo_ref, *rest, final):
    if final:
        gf_ref, o_ref, u_ref, acc_ref = rest
    else:
        o_ref, u_ref, acc_ref = rest
    f = pl.program_id(1)

    @pl.when(f == 0)
    def _():
        u_ref[...] = _rmsnorm(h_ref[...], g_ref[...]).astype(BF16)
        acc_ref[...] = jnp.zeros_like(acc_ref)

    u = u_ref[...]
    gate = _dot(u, wg_ref[...])
    up = _dot(u, wu_ref[...])
    act = (gate * _sigmoid(gate) * up).astype(BF16)
    acc_ref[...] += _dot(act, wo_ref[...])

    @pl.when(f == pl.num_programs(1) - 1)
    def _():
        out = h_ref[...] + acc_ref[...]
        if final:
            out = _rmsnorm(out, gf_ref[...])
        o_ref[...] = out


def _ffn(h, gain_all, win_all, wout_all, layer, tm, tf, final_gain=None):
    m, d = h.shape
    dff = wout_all.shape[1]
    nf = dff // tf
    final = final_gain is not None
    in_specs = [
        pl.BlockSpec((tm, d), lambda i, f: (i, 0)),
        pl.BlockSpec((None, 1, d), lambda i, f: (layer, 0, 0)),
        pl.BlockSpec((None, d, tf), lambda i, f: (layer, 0, f)),
        pl.BlockSpec((None, d, tf), lambda i, f: (layer, 0, nf + f)),
        pl.BlockSpec((None, tf, d), lambda i, f: (layer, f, 0)),
    ]
    args = [h, gain_all, win_all, win_all, wout_all]
    if final:
        in_specs.append(pl.BlockSpec((1, d), lambda i, f: (0, 0)))
        args.append(final_gain)
    return pl.pallas_call(
        functools.partial(_ffn_kernel, final=final),
        grid=(m // tm, nf),
        in_specs=in_specs,
        out_specs=pl.BlockSpec((tm, d), lambda i, f: (i, 0)),
        out_shape=jax.ShapeDtypeStruct((m, d), F32),
        scratch_shapes=[pltpu.VMEM((tm, d), BF16), pltpu.VMEM((tm, d), F32)],
        compiler_params=_params(("parallel", "arbitrary")),
        name="ffn",
    )(*args)


def kernel(x, mem, w_in, w_pool_group, pool_scale, hgrn_lower_bounds, hgrn_norm, w_branch_pool,
           w_branch_hgrn, w_mix_out, norm_mix, norm_mem, norm_cross, w_xq, w_xkv, w_xo, norm_ffn,
           w_ffn_in, w_ffn_out, norm_final):
    batch, seq, d = x.shape
    mem_len = mem.shape[1]
    depth = w_in.shape[0]
    pd = w_branch_pool.shape[1]
    hd = hgrn_lower_bounds.shape[1]

    f_lo, f_hi = pd + hd, pd + 2 * hd
    w_f = w_in[:, :, f_lo:f_hi].astype(BF16)
    w_rest = jnp.concatenate([w_in[:, :, :f_lo], w_in[:, :, f_hi:]], axis=2).astype(BF16)
    wg = w_pool_group.astype(BF16)
    wbp = w_branch_pool.astype(BF16)
    wbh = w_branch_hgrn.astype(BF16)
    wmo = w_mix_out.astype(BF16)
    wxq = w_xq.astype(BF16)
    wxkv = w_xkv.astype(BF16)
    wxo = w_xo.astype(BF16)
    wfi = w_ffn_in.astype(BF16)
    wfo = w_ffn_out.astype(BF16)

    as_rows = lambda a: a.reshape(a.shape[0], 1, a.shape[1])
    g_mix, g_mem, g_cross, g_ffn = map(as_rows, (norm_mix, norm_mem, norm_cross, norm_ffn))
    g_hgrn, sc_pool = as_rows(hgrn_norm), as_rows(pool_scale)
    g_final = norm_final.reshape(1, d)

    mall_np, lev_np, nlev = _hgrn_tables(HGRN_CHUNK)
    mall = jnp.asarray(mall_np, BF16)
    lev = jnp.asarray(lev_np)

    h = x.reshape(batch * seq, d)
    mem2 = mem.reshape(batch * mem_len, d)
    for l in range(depth):
        zf = _norm_mm(h, g_mix, w_f, l, F32, 1024, 1024)
        zr = _norm_mm(h, g_mix, w_rest, l, BF16, 1024, 1024)
        b_out = _hgrn(zr, zf, hgrn_lower_bounds, g_hgrn, l, batch, seq, mall, lev, nlev)
        h = _mix(zr, b_out, h, wg, sc_pool, wbp, wbh, wmo, l, batch, seq, 256)
        kv = _norm_mm(mem2, g_mem, wxkv, l, BF16, 1024, 1024)
        h = _xattn(h, g_cross, wxq, kv, wxo, l, batch, seq, mem_len, 256)
        h = _ffn(h, g_ffn, wfi, wfo, l, 512, 512, g_final if l == depth - 1 else None)
    return h.reshape(batch, seq, d)
```

```python
import functools

import numpy as np
import jax
import jax.numpy as jnp
from jax import lax
from jax.experimental import pallas as pl
from jax.experimental.pallas import tpu as pltpu

F32 = jnp.float32
BF16 = jnp.bfloat16
EPS = 1e-6
LOG2E = 1.4426950408889634

SUBLANES = 8

POOL_WINDOWS = (2, 4, 8, 16)
POOL_HALO = 16
HGRN_HEAD = 128
HGRN_CHUNK = 128
X_HEADS = 4

VMEM_LIMIT = 56 * 1024 * 1024


def _params(sem):
    return pltpu.CompilerParams(dimension_semantics=sem, vmem_limit_bytes=VMEM_LIMIT)


def _resident(block_shape, index_map):
    return pl.BlockSpec(block_shape, index_map, pipeline_mode=pl.Buffered(1))


def _rmsnorm(x, gain):
    ms = jnp.mean(x * x, axis=-1, keepdims=True)
    return x * lax.rsqrt(ms + EPS) * gain


def _dot(a, b):
    return jnp.dot(a, b, preferred_element_type=F32)


def _dot_nt(a, b):
    return lax.dot_general(a, b, (((1,), (1,)), ((), ())), preferred_element_type=F32)


def _dot_tn(a, b):
    return lax.dot_general(a, b, (((0,), (0,)), ((), ())), preferred_element_type=F32)


def _sigmoid(x):
    return 1.0 / (1.0 + jnp.exp(-x))


def _norm_mm_kernel(x_ref, g_ref, w_ref, o_ref, u_ref):
    @pl.when(pl.program_id(1) == 0)
    def _():
        u_ref[...] = _rmsnorm(x_ref[...], g_ref[...]).astype(BF16)

    o_ref[...] = _dot(u_ref[...], w_ref[...]).astype(o_ref.dtype)


def _norm_mm(x, gain_all, w_all, layer, out_dtype, tm, tn):
    m, d = x.shape
    n = w_all.shape[-1]
    return pl.pallas_call(
        _norm_mm_kernel,
        grid=(m // tm, n // tn),
        in_specs=[
            pl.BlockSpec((tm, d), lambda i, j: (i, 0)),
            pl.BlockSpec((None, 1, d), lambda i, j: (layer, 0, 0)),
            pl.BlockSpec((None, d, tn), lambda i, j: (layer, 0, j)),
        ],
        out_specs=pl.BlockSpec((tm, tn), lambda i, j: (i, j)),
        out_shape=jax.ShapeDtypeStruct((m, n), out_dtype),
        scratch_shapes=[pltpu.VMEM((tm, d), BF16)],
        compiler_params=_params(("parallel", "arbitrary")),
        name="norm_mm",
    )(x, gain_all, w_all)


def _in_proj_kernel(x_ref, g_ref, w_ref, zr_ref, zf_ref, u_ref, *, f_blk):
    j = pl.program_id(1)

    @pl.when(j == 0)
    def _():
        u_ref[...] = _rmsnorm(x_ref[...], g_ref[...]).astype(BF16)

    z = _dot(u_ref[...], w_ref[...])

    @pl.when(j == f_blk)
    def _():
        zf_ref[...] = z

    @pl.when(j != f_blk)
    def _():
        zr_ref[...] = z.astype(BF16)


def _in_proj(x, gain_all, w_all, layer, tm, tn, f_lo):
    m, d = x.shape
    n = w_all.shape[-1]
    f_blk = f_lo // tn
    return pl.pallas_call(
        functools.partial(_in_proj_kernel, f_blk=f_blk),
        grid=(m // tm, n // tn),
        in_specs=[
            pl.BlockSpec((tm, d), lambda i, j: (i, 0)),
            pl.BlockSpec((None, 1, d), lambda i, j: (layer, 0, 0)),
            pl.BlockSpec((None, d, tn), lambda i, j: (layer, 0, j)),
        ],
        out_specs=[
            pl.BlockSpec((tm, tn), lambda i, j: (i, j - (j >= f_blk).astype(jnp.int32))),
            pl.BlockSpec((tm, tn), lambda i, j: (i, 0)),
        ],
        out_shape=[jax.ShapeDtypeStruct((m, n - tn), BF16), jax.ShapeDtypeStruct((m, tn), F32)],
        scratch_shapes=[pltpu.VMEM((tm, d), BF16)],
        compiler_params=_params(("parallel", "arbitrary")),
        name="in_proj",
    )(x, gain_all, w_all)


def _hgrn_tables(c):
    nlev = int(np.log2(c))
    t = np.arange(c)[:, None]
    j = np.arange(c)[None, :]
    blocks = [(j <= t)]
    for l in range(1, nlev + 1):
        m = 2 ** (l - 1)
        r = (t // (2 * m)) * (2 * m) + m - 1
        right = (t % (2 * m)) >= m
        blocks.append(np.where(right, (j > r) & (j <= t), (j > t) & (j <= r)))
    blocks.append(j > t)
    mall = np.concatenate(blocks, axis=0).astype(np.float32)
    mall = np.concatenate([mall, mall], axis=1)
    x = t ^ j
    lev = np.zeros((c, c), np.int32)
    low = j < t
    lev[low] = np.floor(np.log2(x[low])).astype(np.int32) + 1
    lev[np.arange(c), np.arange(c)] = nlev + 1
    return mall, lev, nlev


def _silu(z):
    return z / (1.0 + jnp.exp(-z))


def _hgrn_kernel(lb_ref, q_ref, f_ref, i_ref, og_ref, gain_ref, mall_ref, lev_ref,
                 o_ref, state_ref, lbs_ref, *, layer, nlev, heads):
    c = HGRN_CHUNK
    hw = HGRN_HEAD
    pw = 2 * hw

    @pl.when(pl.program_id(1) == 0)
    def _():
        state_ref[...] = jnp.zeros_like(state_ref)
        x = lb_ref[...]
        e = jnp.exp(x - jnp.max(x, axis=0, keepdims=True))
        sm = e / jnp.sum(e, axis=0, keepdims=True)
        lb = jnp.zeros_like(sm[0:1])
        for r in range(1, layer + 1):
            lb = lb + sm[r:r + 1]
        lbs_ref[0:1, :] = jnp.log2(lb)
        lbs_ref[1:2, :] = jnp.log2(1.0 - lb)
        lbs_ref[2:3, :] = 1.0 - lb

    sub = SUBLANES
    nblk = c // sub
    lev = lev_ref[...]
    lev_blk = [lev[i * sub:(i + 1) * sub] for i in range(nblk)]
    row_in_blk = lax.broadcasted_iota(jnp.int32, (sub, hw), 0)

    def is_right(row, m):
        return (row & m) != 0

    def one_head(q_h, k_h, w_h, v, hidx):
        q_rows = [q_h[i * sub:(i + 1) * sub] for i in range(nblk)]
        k_rows = [k_h[i * sub:(i + 1) * sub] for i in range(nblk)]
        s0 = _dot_nt(q_h.astype(BF16), k_h.astype(BF16))
        s_rows = [jnp.where(lev_blk[i] == nlev + 1, s0[i * sub:(i + 1) * sub], 0.0) for i in range(nblk)]
        for l in range(1, nlev + 1):
            m = 2 ** (l - 1)
            if m >= sub:
                qk = [q_rows[i] if is_right(i * sub, m) else k_rows[i] for i in range(nblk)]
            else:
                pick_q = is_right(row_in_blk, m)
                qk = [jnp.where(pick_q, q_rows[i], k_rows[i]) for i in range(nblk)]
            x = (jnp.concatenate(qk, axis=0) * w_h[l * c:(l + 1) * c]).astype(BF16)
            s_l = _dot_nt(x, x)
            for i in range(nblk):
                if m < sub or is_right(i * sub, m):
                    s_rows[i] = jnp.where(lev_blk[i] == l, s_l[i * sub:(i + 1) * sub], s_rows[i])
        s = jnp.concatenate(s_rows, axis=0).astype(BF16)
        state_t = state_ref[hidx]
        o = _dot(s, v) + _dot_nt((q_h * w_h[0:c]).astype(BF16), state_t.astype(BF16))
        k_dec = (k_h * w_h[(nlev + 1) * c:(nlev + 2) * c]).astype(BF16)
        state_ref[hidx] = state_t * w_h[c - 1:c] + _dot_tn(v, k_dec)
        return o * lax.rsqrt(jnp.mean(o * o, axis=1, keepdims=True) + EPS)

    for hp in range(heads // 2):
        ps = slice(hp * pw, (hp + 1) * pw)
        q = _silu(q_ref[:, ps]).astype(F32)
        log2_lb = lbs_ref[0:1, ps]
        log2_1m_lb = lbs_ref[1:2, ps]
        one_m_lb = lbs_ref[2:3, ps]
        z2 = f_ref[:, ps] * LOG2E
        e = jnp.exp2(-jnp.abs(z2))
        ope = 1.0 + e
        rcp = 1.0 / ope
        k = one_m_lb * jnp.where(z2 >= 0, e * rcp, rcp)
        cc = log2_1m_lb + (jnp.minimum(z2, 0.0) - jnp.log2(ope))
        g = jnp.maximum(log2_lb, cc) + jnp.log2(1.0 + jnp.exp2(-jnp.abs(log2_lb - cc)))
        g_hi = g.astype(BF16)
        g_lo = (g - g_hi.astype(F32)).astype(BF16)
        w = jnp.exp2(_dot(mall_ref[...], jnp.concatenate([g_hi, g_lo], axis=0)))
        outs = []
        for j in range(2):
            cols = slice(j * hw, (j + 1) * hw)
            v = i_ref[:, hp * pw + j * hw:hp * pw + (j + 1) * hw]
            outs.append(one_head(q[:, cols], k[:, cols], w[:, cols], v, 2 * hp + j))
        o = (jnp.concatenate(outs, axis=1) * gain_ref[:, ps]).astype(BF16)
        o_ref[:, ps] = o * _silu(og_ref[:, ps])


def _hgrn(zr, zf, lb_raw, gain_all, layer, batch, seq, mall, lev, nlev):
    m = zr.shape[0]
    hd = zf.shape[1]
    depth = lb_raw.shape[0]
    c = HGRN_CHUNK
    nc = seq // c
    heads = hd // HGRN_HEAD
    row = lambda b, t: b * nc + t
    return pl.pallas_call(
        functools.partial(_hgrn_kernel, layer=layer, nlev=nlev, heads=heads),
        grid=(batch, nc),
        in_specs=[
            _resident((depth, hd), lambda b, t: (0, 0)),
            pl.BlockSpec((c, hd), lambda b, t: (row(b, t), 1)),
            pl.BlockSpec((c, hd), lambda b, t: (row(b, t), 0)),
            pl.BlockSpec((c, hd), lambda b, t: (row(b, t), 2)),
            pl.BlockSpec((c, hd), lambda b, t: (row(b, t), 3)),
            pl.BlockSpec((None, 1, hd), lambda b, t: (layer, 0, 0)),
            _resident(mall.shape, lambda b, t: (0, 0)),
            _resident(lev.shape, lambda b, t: (0, 0)),
        ],
        out_specs=pl.BlockSpec((c, hd), lambda b, t: (row(b, t), 0)),
        out_shape=jax.ShapeDtypeStruct((m, hd), BF16),
        scratch_shapes=[pltpu.VMEM((heads, HGRN_HEAD, HGRN_HEAD), F32),
                        pltpu.VMEM((8, hd), F32)],
        compiler_params=_params(("parallel", "arbitrary")),
        name="hgrn",
    )(lb_raw, zr, zf, zr, zr, gain_all, mall, lev)


def _mix_kernel(zp_ref, halo_ref, ga_ref, gb_ref, bo_ref, h_ref, wg_ref, sc_ref,
                wbp_ref, wbh_ref, wmo_ref, o_ref, *, tm):
    t0 = pl.program_id(1) * tm
    x = zp_ref[...].astype(F32)
    halo = jnp.where(t0 > 0, halo_ref[...].astype(F32), 0.0)
    ext = jnp.concatenate([halo, x], axis=0)
    pos = lax.broadcasted_iota(jnp.int32, (tm, 1), 0) + t0
    gd = x.shape[1] // len(POOL_WINDOWS)
    outs = []
    for g, wdw in enumerate(POOL_WINDOWS):
        cols = slice(g * gd, (g + 1) * gd)
        e = ext[:, cols]
        sh = 1
        while sh < wdw:
            e = e + pltpu.roll(e, sh, axis=0)
            sh *= 2
        cnt = jnp.minimum(pos + 1, wdw).astype(F32)
        pooled = (e[POOL_HALO:] / cnt - x[:, cols]).astype(BF16)
        outs.append((_dot(pooled, wg_ref[g]) * sc_ref[:, cols]).astype(BF16))
    a_out = jnp.concatenate(outs, axis=1)
    merged = _sigmoid(ga_ref[...].astype(F32)) * _dot(a_out, wbp_ref[...])
    merged = merged + _sigmoid(gb_ref[...].astype(F32)) * _dot(bo_ref[...], wbh_ref[...])
    o_ref[...] = h_ref[...] + _dot(merged.astype(BF16), wmo_ref[...])


def _mix(zr, b_out, h, wg_all, sc_all, wbp_all, wbh_all, wmo_all, layer, batch, seq, tm):
    m, d = h.shape
    pd = b_out.shape[1]
    ng, gd = wg_all.shape[1], wg_all.shape[2]
    nt = seq // tm
    row = lambda b, t: b * nt + t
    hb = tm // POOL_HALO
    return pl.pallas_call(
        functools.partial(_mix_kernel, tm=tm),
        grid=(batch, nt),
        in_specs=[
            pl.BlockSpec((tm, pd), lambda b, t: (row(b, t), 0)),
            pl.BlockSpec((POOL_HALO, pd), lambda b, t: (jnp.maximum(row(b, t) * hb - 1, 0), 0)),
            pl.BlockSpec((tm, d), lambda b, t: (row(b, t), 2)),
            pl.BlockSpec((tm, d), lambda b, t: (row(b, t), 3)),
            pl.BlockSpec((tm, pd), lambda b, t: (row(b, t), 0)),
            pl.BlockSpec((tm, d), lambda b, t: (row(b, t), 0)),
            _resident((None, ng, gd, gd), lambda b, t: (layer, 0, 0, 0)),
            _resident((None, 1, pd), lambda b, t: (layer, 0, 0)),
            _resident((None, pd, d), lambda b, t: (layer, 0, 0)),
            _resident((None, pd, d), lambda b, t: (layer, 0, 0)),
            _resident((None, d, d), lambda b, t: (layer, 0, 0)),
        ],
        out_specs=pl.BlockSpec((tm, d), lambda b, t: (row(b, t), 0)),
        out_shape=jax.ShapeDtypeStruct((m, d), F32),
        compiler_params=_params(("parallel", "parallel")),
        name="mix",
    )(zr, zr, zr, zr, b_out, h, wg_all, sc_all, wbp_all, wbh_all, wmo_all)


def _xattn_kernel(h_ref, g_ref, wq_ref, kv_ref, wo_ref, o_ref, att_ref):
    h = h_ref[...]
    d = h.shape[1]
    dh = d // X_HEADS
    u = _rmsnorm(h, g_ref[...]).astype(BF16)
    q = _dot(u, wq_ref[...]).astype(BF16)
    for hh in range(X_HEADS):
        cols = slice(hh * dh, (hh + 1) * dh)
        s = _dot_nt(q[:, cols], kv_ref[:, cols]) * (dh ** -0.5)
        p = jnp.exp(s - jnp.max(s, axis=-1, keepdims=True))
        p = p / jnp.sum(p, axis=-1, keepdims=True)
        att_ref[:, cols] = _dot(p.astype(BF16), kv_ref[:, d + hh * dh:d + (hh + 1) * dh]).astype(BF16)
    o_ref[...] = h + _dot(att_ref[...], wo_ref[...])


def _xattn(h, gain_all, wq_all, kv, wo_all, layer, batch, seq, mem_len, tm):
    m, d = h.shape
    nt = seq // tm
    row = lambda b, t: b * nt + t
    return pl.pallas_call(
        _xattn_kernel,
        grid=(batch, nt),
        in_specs=[
            pl.BlockSpec((tm, d), lambda b, t: (row(b, t), 0)),
            pl.BlockSpec((None, 1, d), lambda b, t: (layer, 0, 0)),
            _resident((None, d, d), lambda b, t: (layer, 0, 0)),
            pl.BlockSpec((mem_len, 2 * d), lambda b, t: (b, 0)),
            _resident((None, d, d), lambda b, t: (layer, 0, 0)),
        ],
        out_specs=pl.BlockSpec((tm, d), lambda b, t: (row(b, t), 0)),
        out_shape=jax.ShapeDtypeStruct((m, d), F32),
        scratch_shapes=[pltpu.VMEM((tm, d), BF16)],
        compiler_params=_params(("parallel", "parallel")),
        name="xattn",
    )(h, gain_all, wq_all, kv, wo_all)


def _ffn_kernel(h_ref, g_ref, wg_ref, wu_ref, wo_ref, *rest, final):
    if final:
        gf_ref, o_ref, u_ref, acc_ref = rest
    else:
        o_ref, u_ref, acc_ref = rest
    f = pl.program_id(1)

    @pl.when(f == 0)
    def _():
        u_ref[...] = _rmsnorm(h_ref[...], g_ref[...]).astype(BF16)
        acc_ref[...] = jnp.zeros_like(acc_ref)

    u = u_ref[...]
    gate = _dot(u, wg_ref[...])
    up = _dot(u, wu_ref[...])
    act = (gate * _sigmoid(gate) * up).astype(BF16)
    acc_ref[...] += _dot(act, wo_ref[...])

    @pl.when(f == pl.num_programs(1) - 1)
    def _():
        out = h_ref[...] + acc_ref[...]
        if final:
            out = _rmsnorm(out, gf_ref[...])
        o_ref[...] = out


def _ffn(h, gain_all, win_all, wout_all, layer, tm, tf, final_gain=None):
    m, d = h.shape
    dff = wout_all.shape[1]
    nf = dff // tf
    final = final_gain is not None
    in_specs = [
        pl.BlockSpec((tm, d), lambda i, f: (i, 0)),
        pl.BlockSpec((None, 1, d), lambda i, f: (layer, 0, 0)),
        pl.BlockSpec((None, d, tf), lambda i, f: (layer, 0, f)),
        pl.BlockSpec((None, d, tf), lambda i, f: (layer, 0, nf + f)),
        pl.BlockSpec((None, tf, d), lambda i, f: (layer, f, 0)),
    ]
    args = [h, gain_all, win_all, win_all, wout_all]
    if final:
        in_specs.append(pl.BlockSpec((1, d), lambda i, f: (0, 0)))
        args.append(final_gain)
    return pl.pallas_call(
        functools.partial(_ffn_kernel, final=final),
        grid=(m // tm, nf),
        in_specs=in_specs,
        out_specs=pl.BlockSpec((tm, d), lambda i, f: (i, 0)),
        out_shape=jax.ShapeDtypeStruct((m, d), F32),
        scratch_shapes=[pltpu.VMEM((tm, d), BF16), pltpu.VMEM((tm, d), F32)],
        compiler_params=_params(("parallel", "arbitrary")),
        name="ffn",
    )(*args)


def kernel(x, mem, w_in, w_pool_group, pool_scale, hgrn_lower_bounds, hgrn_norm, w_branch_pool,
           w_branch_hgrn, w_mix_out, norm_mix, norm_mem, norm_cross, w_xq, w_xkv, w_xo, norm_ffn,
           w_ffn_in, w_ffn_out, norm_final):
    batch, seq, d = x.shape
    mem_len = mem.shape[1]
    depth = w_in.shape[0]
    pd = w_branch_pool.shape[1]
    hd = hgrn_lower_bounds.shape[1]

    f_lo = pd + hd
    win = w_in.astype(BF16)
    wg = w_pool_group.astype(BF16)
    wbp = w_branch_pool.astype(BF16)
    wbh = w_branch_hgrn.astype(BF16)
    wmo = w_mix_out.astype(BF16)
    wxq = w_xq.astype(BF16)
    wxkv = w_xkv.astype(BF16)
    wxo = w_xo.astype(BF16)
    wfi = w_ffn_in.astype(BF16)
    wfo = w_ffn_out.astype(BF16)

    as_rows = lambda a: a.reshape(a.shape[0], 1, a.shape[1])
    g_mix, g_mem, g_cross, g_ffn = map(as_rows, (norm_mix, norm_mem, norm_cross, norm_ffn))
    g_hgrn, sc_pool = as_rows(hgrn_norm), as_rows(pool_scale)
    g_final = norm_final.reshape(1, d)

    mall_np, lev_np, nlev = _hgrn_tables(HGRN_CHUNK)
    mall = jnp.asarray(mall_np, BF16)
    lev = jnp.asarray(lev_np)

    h = x.reshape(batch * seq, d)
    mem2 = mem.reshape(batch * mem_len, d)
    for l in range(depth):
        zr, zf = _in_proj(h, g_mix, win, l, 1024, hd, f_lo)
        b_out = _hgrn(zr, zf, hgrn_lower_bounds, g_hgrn, l, batch, seq, mall, lev, nlev)
        h = _mix(zr, b_out, h, wg, sc_pool, wbp, wbh, wmo, l, batch, seq, 256)
        kv = _norm_mm(mem2, g_mem, wxkv, l, BF16, 1024, 1024)
        h = _xattn(h, g_cross, wxq, kv, wxo, l, batch, seq, mem_len, 256)
        h = _ffn(h, g_ffn, wfi, wfo, l, 512, 512, g_final if l == depth - 1 else None)
    return h.reshape(batch, seq, d)
```

```python
import functools

import numpy as np
import jax
import jax.numpy as jnp
from jax import lax
from jax.experimental import pallas as pl
from jax.experimental.pallas import tpu as pltpu

F32 = jnp.float32
BF16 = jnp.bfloat16
EPS = 1e-6
LOG2E = 1.4426950408889634

SUBLANES = 8

POOL_WINDOWS = (2, 4, 8, 16)
POOL_HALO = 16
HGRN_HEAD = 128
HGRN_CHUNK = 128
X_HEADS = 4

VMEM_LIMIT = 56 * 1024 * 1024


def _params(sem):
    return pltpu.CompilerParams(dimension_semantics=sem, vmem_limit_bytes=VMEM_LIMIT)


def _resident(block_shape, index_map):
    return pl.BlockSpec(block_shape, index_map, pipeline_mode=pl.Buffered(1))


def _rmsnorm(x, gain):
    ms = jnp.mean(x * x, axis=-1, keepdims=True)
    return x * lax.rsqrt(ms + EPS) * gain


def _dot(a, b):
    return jnp.dot(a, b, preferred_element_type=F32)


def _dot_nt(a, b):
    return lax.dot_general(a, b, (((1,), (1,)), ((), ())), preferred_element_type=F32)


def _dot_tn(a, b):
    return lax.dot_general(a, b, (((0,), (0,)), ((), ())), preferred_element_type=F32)


def _sigmoid(x):
    return 1.0 / (1.0 + jnp.exp(-x))


def _norm_mm_kernel(x_ref, g_ref, w_ref, o_ref, u_ref):
    @pl.when(pl.program_id(1) == 0)
    def _():
        u_ref[...] = _rmsnorm(x_ref[...], g_ref[...]).astype(BF16)

    o_ref[...] = _dot(u_ref[...], w_ref[...]).astype(o_ref.dtype)


def _norm_mm(x, gain_all, w_all, layer, out_dtype, tm, tn):
    m, d = x.shape
    n = w_all.shape[-1]
    return pl.pallas_call(
        _norm_mm_kernel,
        grid=(m // tm, n // tn),
        in_specs=[
            pl.BlockSpec((tm, d), lambda i, j: (i, 0)),
            pl.BlockSpec((None, 1, d), lambda i, j: (layer, 0, 0)),
            pl.BlockSpec((None, d, tn), lambda i, j: (layer, 0, j)),
        ],
        out_specs=pl.BlockSpec((tm, tn), lambda i, j: (i, j)),
        out_shape=jax.ShapeDtypeStruct((m, n), out_dtype),
        scratch_shapes=[pltpu.VMEM((tm, d), BF16)],
        compiler_params=_params(("parallel", "arbitrary")),
        name="norm_mm",
    )(x, gain_all, w_all)


def _in_proj_kernel(x_ref, g_ref, w_ref, zr_ref, zf_ref, u_ref, *, f_blk):
    j = pl.program_id(1)

    @pl.when(j == 0)
    def _():
        u_ref[...] = _rmsnorm(x_ref[...], g_ref[...]).astype(BF16)

    z = _dot(u_ref[...], w_ref[...])

    @pl.when(j == f_blk)
    def _():
        zf_ref[...] = z

    @pl.when(j != f_blk)
    def _():
        zr_ref[...] = z.astype(BF16)


def _in_proj(x, gain_all, w_all, layer, tm, tn, f_lo):
    m, d = x.shape
    n = w_all.shape[-1]
    f_blk = f_lo // tn
    return pl.pallas_call(
        functools.partial(_in_proj_kernel, f_blk=f_blk),
        grid=(m // tm, n // tn),
        in_specs=[
            pl.BlockSpec((tm, d), lambda i, j: (i, 0)),
            pl.BlockSpec((None, 1, d), lambda i, j: (layer, 0, 0)),
            pl.BlockSpec((None, d, tn), lambda i, j: (layer, 0, j)),
        ],
        out_specs=[
            pl.BlockSpec((tm, tn), lambda i, j: (i, j - (j >= f_blk).astype(jnp.int32))),
            pl.BlockSpec((tm, tn), lambda i, j: (i, 0)),
        ],
        out_shape=[jax.ShapeDtypeStruct((m, n - tn), BF16), jax.ShapeDtypeStruct((m, tn), F32)],
        scratch_shapes=[pltpu.VMEM((tm, d), BF16)],
        compiler_params=_params(("parallel", "arbitrary")),
        name="in_proj",
    )(x, gain_all, w_all)


def _hgrn_tables(c):
    nlev = int(np.log2(c))
    t = np.arange(c)[:, None]
    j = np.arange(c)[None, :]
    blocks = [(j <= t)]
    for l in range(1, nlev + 1):
        m = 2 ** (l - 1)
        r = (t // (2 * m)) * (2 * m) + m - 1
        right = (t % (2 * m)) >= m
        blocks.append(np.where(right, (j > r) & (j <= t), (j > t) & (j <= r)))
    blocks.append(j > t)
    mall = np.concatenate(blocks, axis=0).astype(np.float32)
    mall = np.concatenate([mall, mall], axis=1)
    x = t ^ j
    lev = np.zeros((c, c), np.int32)
    low = j < t
    lev[low] = np.floor(np.log2(x[low])).astype(np.int32) + 1
    lev[np.arange(c), np.arange(c)] = nlev + 1
    return mall, lev, nlev


def _silu(z):
    return z / (1.0 + jnp.exp(-z))


def _hgrn_kernel(lb_ref, q_ref, f_ref, i_ref, og_ref, gain_ref, mall_ref, lev_ref,
                 o_ref, state_ref, lbs_ref, *, layer, nlev, heads, rows):
    c = HGRN_CHUNK
    hw = HGRN_HEAD
    pw = 2 * hw

    @pl.when(pl.program_id(1) == 0)
    def _():
        state_ref[...] = jnp.zeros_like(state_ref)
        x = lb_ref[...]
        e = jnp.exp(x - jnp.max(x, axis=0, keepdims=True))
        sm = e / jnp.sum(e, axis=0, keepdims=True)
        lb = jnp.zeros_like(sm[0:1])
        for r in range(1, layer + 1):
            lb = lb + sm[r:r + 1]
        lbs_ref[0:1, :] = jnp.log2(lb)
        lbs_ref[1:2, :] = jnp.log2(1.0 - lb)
        lbs_ref[2:3, :] = 1.0 - lb

    sub = SUBLANES
    nblk = c // sub
    lev = lev_ref[...]
    lev_blk = [lev[i * sub:(i + 1) * sub] for i in range(nblk)]
    row_in_blk = lax.broadcasted_iota(jnp.int32, (sub, hw), 0)

    def is_right(row, m):
        return (row & m) != 0

    def level_operand(q_rows, k_rows, w_l, m):
        if m >= sub:
            qk = [q_rows[i] if is_right(i * sub, m) else k_rows[i] for i in range(nblk)]
        else:
            pick_q = is_right(row_in_blk, m)
            qk = [jnp.where(pick_q, q_rows[i], k_rows[i]) for i in range(nblk)]
        return (jnp.concatenate(qk, axis=0) * w_l).astype(BF16)

    def one_head(q_h, k_h, w_h, v, hidx):
        q_rows = [q_h[i * sub:(i + 1) * sub] for i in range(nblk)]
        k_rows = [k_h[i * sub:(i + 1) * sub] for i in range(nblk)]
        s0 = _dot_nt(q_h.astype(BF16), k_h.astype(BF16))
        s_rows = [jnp.where(lev_blk[i] == nlev + 1, s0[i * sub:(i + 1) * sub], 0.0) for i in range(nblk)]
        for l in range(1, nlev + 1):
            m = 2 ** (l - 1)
            x = level_operand(q_rows, k_rows, w_h[l * c:(l + 1) * c], m)
            s_l = _dot_nt(x, x)
            for i in range(nblk):
                if m < sub or is_right(i * sub, m):
                    s_rows[i] = jnp.where(lev_blk[i] == l, s_l[i * sub:(i + 1) * sub], s_rows[i])
        s = jnp.concatenate(s_rows, axis=0).astype(BF16)
        state_t = state_ref[hidx]
        o = _dot(s, v) + _dot_nt((q_h * w_h[0:c]).astype(BF16), state_t.astype(BF16))
        k_dec = (k_h * w_h[(nlev + 1) * c:(nlev + 2) * c]).astype(BF16)
        state_ref[hidx] = state_t * w_h[c - 1:c] + _dot_tn(v, k_dec)
        return o * lax.rsqrt(jnp.mean(o * o, axis=1, keepdims=True) + EPS)

    def chunk(ci, carry):
        rs = pl.ds(pl.multiple_of(ci * c, c), c)
        for hp in range(heads // 2):
            ps = slice(hp * pw, (hp + 1) * pw)
            q = _silu(q_ref[rs, ps]).astype(F32)
            log2_lb = lbs_ref[0:1, ps]
            log2_1m_lb = lbs_ref[1:2, ps]
            one_m_lb = lbs_ref[2:3, ps]
            z2 = f_ref[rs, ps] * LOG2E
            e = jnp.exp2(-jnp.abs(z2))
            ope = 1.0 + e
            rcp = 1.0 / ope
            k = one_m_lb * jnp.where(z2 >= 0, e * rcp, rcp)
            cc = log2_1m_lb + (jnp.minimum(z2, 0.0) - jnp.log2(ope))
            g = jnp.maximum(log2_lb, cc) + jnp.log2(1.0 + jnp.exp2(-jnp.abs(log2_lb - cc)))
            g_hi = g.astype(BF16)
            g_lo = (g - g_hi.astype(F32)).astype(BF16)
            w = jnp.exp2(_dot(mall_ref[...], jnp.concatenate([g_hi, g_lo], axis=0)))
            outs = []
            for j in range(2):
                cols = slice(j * hw, (j + 1) * hw)
                v = i_ref[rs, hp * pw + j * hw:hp * pw + (j + 1) * hw]
                outs.append(one_head(q[:, cols], k[:, cols], w[:, cols], v, 2 * hp + j))
            o = (jnp.concatenate(outs, axis=1) * gain_ref[:, ps]).astype(BF16)
            o_ref[rs, ps] = o * _silu(og_ref[rs, ps])
        return carry

    lax.fori_loop(0, rows // c, chunk, 0)


def _hgrn(zr, zf, lb_raw, gain_all, layer, batch, seq, mall, lev, nlev, rows):
    m = zr.shape[0]
    hd = zf.shape[1]
    depth = lb_raw.shape[0]
    c = rows
    nc = seq // rows
    heads = hd // HGRN_HEAD
    row = lambda b, t: b * nc + t
    return pl.pallas_call(
        functools.partial(_hgrn_kernel, layer=layer, nlev=nlev, heads=heads, rows=rows),
        grid=(batch, nc),
        in_specs=[
            _resident((depth, hd), lambda b, t: (0, 0)),
            pl.BlockSpec((c, hd), lambda b, t: (row(b, t), 1)),
            pl.BlockSpec((c, hd), lambda b, t: (row(b, t), 0)),
            pl.BlockSpec((c, hd), lambda b, t: (row(b, t), 2)),
            pl.BlockSpec((c, hd), lambda b, t: (row(b, t), 3)),
            pl.BlockSpec((None, 1, hd), lambda b, t: (layer, 0, 0)),
            _resident(mall.shape, lambda b, t: (0, 0)),
            _resident(lev.shape, lambda b, t: (0, 0)),
        ],
        out_specs=pl.BlockSpec((c, hd), lambda b, t: (row(b, t), 0)),
        out_shape=jax.ShapeDtypeStruct((m, hd), BF16),
        scratch_shapes=[pltpu.VMEM((heads, HGRN_HEAD, HGRN_HEAD), F32),
                        pltpu.VMEM((8, hd), F32)],
        compiler_params=_params(("parallel", "arbitrary")),
        name="hgrn",
    )(lb_raw, zr, zf, zr, zr, gain_all, mall, lev)


def _mix_kernel(zp_ref, halo_ref, ga_ref, gb_ref, bo_ref, h_ref, wg_ref, sc_ref,
                wbp_ref, wbh_ref, wmo_ref, o_ref, *, tm):
    t0 = pl.program_id(1) * tm
    x = zp_ref[...].astype(F32)
    halo = jnp.where(t0 > 0, halo_ref[...].astype(F32), 0.0)
    ext = jnp.concatenate([halo, x], axis=0)
    pos = lax.broadcasted_iota(jnp.int32, (tm, 1), 0) + t0
    gd = x.shape[1] // len(POOL_WINDOWS)
    outs = []
    for g, wdw in enumerate(POOL_WINDOWS):
        cols = slice(g * gd, (g + 1) * gd)
        e = ext[:, cols]
        sh = 1
        while sh < wdw:
            e = e + pltpu.roll(e, sh, axis=0)
            sh *= 2
        cnt = jnp.minimum(pos + 1, wdw).astype(F32)
        pooled = (e[POOL_HALO:] / cnt - x[:, cols]).astype(BF16)
        outs.append((_dot(pooled, wg_ref[g]) * sc_ref[:, cols]).astype(BF16))
    a_out = jnp.concatenate(outs, axis=1)
    merged = _sigmoid(ga_ref[...].astype(F32)) * _dot(a_out, wbp_ref[...])
    merged = merged + _sigmoid(gb_ref[...].astype(F32)) * _dot(bo_ref[...], wbh_ref[...])
    o_ref[...] = h_ref[...] + _dot(merged.astype(BF16), wmo_ref[...])


def _mix(zr, b_out, h, wg_all, sc_all, wbp_all, wbh_all, wmo_all, layer, batch, seq, tm):
    m, d = h.shape
    pd = b_out.shape[1]
    ng, gd = wg_all.shape[1], wg_all.shape[2]
    nt = seq // tm
    row = lambda b, t: b * nt + t
    hb = tm // POOL_HALO
    return pl.pallas_call(
        functools.partial(_mix_kernel, tm=tm),
        grid=(batch, nt),
        in_specs=[
            pl.BlockSpec((tm, pd), lambda b, t: (row(b, t), 0)),
            pl.BlockSpec((POOL_HALO, pd), lambda b, t: (jnp.maximum(row(b, t) * hb - 1, 0), 0)),
            pl.BlockSpec((tm, d), lambda b, t: (row(b, t), 2)),
            pl.BlockSpec((tm, d), lambda b, t: (row(b, t), 3)),
            pl.BlockSpec((tm, pd), lambda b, t: (row(b, t), 0)),
            pl.BlockSpec((tm, d), lambda b, t: (row(b, t), 0)),
            _resident((None, ng, gd, gd), lambda b, t: (layer, 0, 0, 0)),
            _resident((None, 1, pd), lambda b, t: (layer, 0, 0)),
            _resident((None, pd, d), lambda b, t: (layer, 0, 0)),
            _resident((None, pd, d), lambda b, t: (layer, 0, 0)),
            _resident((None, d, d), lambda b, t: (layer, 0, 0)),
        ],
        out_specs=pl.BlockSpec((tm, d), lambda b, t: (row(b, t), 0)),
        out_shape=jax.ShapeDtypeStruct((m, d), F32),
        compiler_params=_params(("parallel", "parallel")),
        name="mix",
    )(zr, zr, zr, zr, b_out, h, wg_all, sc_all, wbp_all, wbh_all, wmo_all)


def _xattn_kernel(h_ref, g_ref, wq_ref, kv_ref, wo_ref, o_ref, att_ref):
    h = h_ref[...]
    d = h.shape[1]
    dh = d // X_HEADS
    u = _rmsnorm(h, g_ref[...]).astype(BF16)
    q = _dot(u, wq_ref[...]).astype(BF16)
    for hh in range(X_HEADS):
        cols = slice(hh * dh, (hh + 1) * dh)
        s = _dot_nt(q[:, cols], kv_ref[:, cols]) * (dh ** -0.5)
        p = jnp.exp(s - jnp.max(s, axis=-1, keepdims=True))
        p = p / jnp.sum(p, axis=-1, keepdims=True)
        att_ref[:, cols] = _dot(p.astype(BF16), kv_ref[:, d + hh * dh:d + (hh + 1) * dh]).astype(BF16)
    o_ref[...] = h + _dot(att_ref[...], wo_ref[...])


def _xattn(h, gain_all, wq_all, kv, wo_all, layer, batch, seq, mem_len, tm):
    m, d = h.shape
    nt = seq // tm
    row = lambda b, t: b * nt + t
    return pl.pallas_call(
        _xattn_kernel,
        grid=(batch, nt),
        in_specs=[
            pl.BlockSpec((tm, d), lambda b, t: (row(b, t), 0)),
            pl.BlockSpec((None, 1, d), lambda b, t: (layer, 0, 0)),
            _resident((None, d, d), lambda b, t: (layer, 0, 0)),
            pl.BlockSpec((mem_len, 2 * d), lambda b, t: (b, 0)),
            _resident((None, d, d), lambda b, t: (layer, 0, 0)),
        ],
        out_specs=pl.BlockSpec((tm, d), lambda b, t: (row(b, t), 0)),
        out_shape=jax.ShapeDtypeStruct((m, d), F32),
        scratch_shapes=[pltpu.VMEM((tm, d), BF16)],
        compiler_params=_params(("parallel", "parallel")),
        name="xattn",
    )(h, gain_all, wq_all, kv, wo_all)


def _ffn_kernel(h_ref, g_ref, wg_ref, wu_ref, wo_ref, *rest, final):
    if final:
        gf_ref, o_ref, u_ref = rest
    else:
        o_ref, u_ref = rest
    f = pl.program_id(1)

    @pl.when(f == 0)
    def _():
        h = h_ref[...]
        u_ref[...] = _rmsnorm(h, g_ref[...]).astype(BF16)
        o_ref[...] = h

    u = u_ref[...]
    gate = _dot(u, wg_ref[...])
    up = _dot(u, wu_ref[...])
    act = (gate * _sigmoid(gate) * up).astype(BF16)
    o_ref[...] += _dot(act, wo_ref[...])

    if final:
        @pl.when(f == pl.num_programs(1) - 1)
        def _():
            o_ref[...] = _rmsnorm(o_ref[...], gf_ref[...])


def _ffn(h, gain_all, win_all, wout_all, layer, tm, tf, final_gain=None):
    m, d = h.shape
    dff = wout_all.shape[1]
    nf = dff // tf
    final = final_gain is not None
    in_specs = [
        pl.BlockSpec((tm, d), lambda i, f: (i, 0)),
        pl.BlockSpec((None, 1, d), lambda i, f: (layer, 0, 0)),
        pl.BlockSpec((None, d, tf), lambda i, f: (layer, 0, f)),
        pl.BlockSpec((None, d, tf), lambda i, f: (layer, 0, nf + f)),
        pl.BlockSpec((None, tf, d), lambda i, f: (layer, f, 0)),
    ]
    args = [h, gain_all, win_all, win_all, wout_all]
    if final:
        in_specs.append(pl.BlockSpec((1, d), lambda i, f: (0, 0)))
        args.append(final_gain)
    return pl.pallas_call(
        functools.partial(_ffn_kernel, final=final),
        grid=(m // tm, nf),
        in_specs=in_specs,
        out_specs=pl.BlockSpec((tm, d), lambda i, f: (i, 0)),
        out_shape=jax.ShapeDtypeStruct((m, d), F32),
        scratch_shapes=[pltpu.VMEM((tm, d), BF16)],
        compiler_params=_params(("parallel", "arbitrary")),
        name="ffn",
    )(*args)


def _tiles(tokens, seq, mem_tokens, hd):
    return {
        "proj_rows": min(1024, tokens),
        "hgrn_rows": min(512, seq),
        "mix_rows": min(256, seq),
        "kv_rows": min(1024, mem_tokens),
        "kv_cols": hd,
        "xattn_rows": min(512, seq),
        "ffn_rows": min(1024, tokens),
        "ffn_cols": 512,
    }


def kernel(x, mem, w_in, w_pool_group, pool_scale, hgrn_lower_bounds, hgrn_norm, w_branch_pool,
           w_branch_hgrn, w_mix_out, norm_mix, norm_mem, norm_cross, w_xq, w_xkv, w_xo, norm_ffn,
           w_ffn_in, w_ffn_out, norm_final):
    batch, seq, d = x.shape
    mem_len = mem.shape[1]
    depth = w_in.shape[0]
    pd = w_branch_pool.shape[1]
    hd = hgrn_lower_bounds.shape[1]

    f_lo = pd + hd
    win = w_in.astype(BF16)
    wg = w_pool_group.astype(BF16)
    wbp = w_branch_pool.astype(BF16)
    wbh = w_branch_hgrn.astype(BF16)
    wmo = w_mix_out.astype(BF16)
    wxq = w_xq.astype(BF16)
    wxkv = w_xkv.astype(BF16)
    wxo = w_xo.astype(BF16)
    wfi = w_ffn_in.astype(BF16)
    wfo = w_ffn_out.astype(BF16)

    as_rows = lambda a: a.reshape(a.shape[0], 1, a.shape[1])
    g_mix, g_mem, g_cross, g_ffn = map(as_rows, (norm_mix, norm_mem, norm_cross, norm_ffn))
    g_hgrn, sc_pool = as_rows(hgrn_norm), as_rows(pool_scale)
    g_final = norm_final.reshape(1, d)

    mall_np, lev_np, nlev = _hgrn_tables(HGRN_CHUNK)
    mall = jnp.asarray(mall_np, BF16)
    lev = jnp.asarray(lev_np)

    t = _tiles(batch * seq, seq, batch * mem_len, hd)
    h = x.reshape(batch * seq, d)
    mem2 = mem.reshape(batch * mem_len, d)
    for l in range(depth):
        zr, zf = _in_proj(h, g_mix, win, l, t["proj_rows"], hd, f_lo)
        b_out = _hgrn(zr, zf, hgrn_lower_bounds, g_hgrn, l, batch, seq, mall, lev, nlev, t["hgrn_rows"])
        h = _mix(zr, b_out, h, wg, sc_pool, wbp, wbh, wmo, l, batch, seq, t["mix_rows"])
        kv = _norm_mm(mem2, g_mem, wxkv, l, BF16, t["kv_rows"], t["kv_cols"])
        h = _xattn(h, g_cross, wxq, kv, wxo, l, batch, seq, mem_len, t["xattn_rows"])
        h = _ffn(h, g_ffn, wfi, wfo, l, t["ffn_rows"], t["ffn_cols"], g_final if l == depth - 1 else None)
    return h.reshape(batch, seq, d)
```

```python
import functools

import numpy as np
import jax
import jax.numpy as jnp
from jax import lax
from jax.experimental import pallas as pl
from jax.experimental.pallas import tpu as pltpu

F32 = jnp.float32
BF16 = jnp.bfloat16
EPS = 1e-6
LOG2E = 1.4426950408889634

SUBLANES = 8
LANES = 128
BF16_ROW_TILE = 16

POOL_WINDOWS = (2, 4, 8, 16)
POOL_HALO = 16
HGRN_HEAD = 128
HGRN_CHUNK = 128
X_HEADS = 4

VMEM_LIMIT = 56 * 1024 * 1024


def _params(sem):
    return pltpu.CompilerParams(dimension_semantics=sem, vmem_limit_bytes=VMEM_LIMIT)


def _resident(block_shape, index_map):
    return pl.BlockSpec(block_shape, index_map, pipeline_mode=pl.Buffered(1))


def _rmsnorm(x, gain):
    ms = jnp.mean(x * x, axis=-1, keepdims=True)
    return x * lax.rsqrt(ms + EPS) * gain


def _dot(a, b):
    return jnp.dot(a, b, preferred_element_type=F32)


def _dot_nt(a, b):
    return lax.dot_general(a, b, (((1,), (1,)), ((), ())), preferred_element_type=F32)


def _dot_tn(a, b):
    return lax.dot_general(a, b, (((0,), (0,)), ((), ())), preferred_element_type=F32)


def _sigmoid(x):
    return 1.0 / (1.0 + jnp.exp(-x))


def _norm_mm_kernel(x_ref, g_ref, w_ref, o_ref, u_ref):
    @pl.when(pl.program_id(1) == 0)
    def _():
        u_ref[...] = _rmsnorm(x_ref[...], g_ref[...]).astype(BF16)

    o_ref[...] = _dot(u_ref[...], w_ref[...]).astype(o_ref.dtype)


def _norm_mm(x, gain_all, w, layer, out_dtype, tm, tn):
    m, d = x.shape
    n = w.shape[-1]
    return pl.pallas_call(
        _norm_mm_kernel,
        grid=(m // tm, n // tn),
        in_specs=[
            pl.BlockSpec((tm, d), lambda i, j: (i, 0)),
            pl.BlockSpec((None, 1, d), lambda i, j: (layer, 0, 0)),
            pl.BlockSpec((d, tn), lambda i, j: (0, j)),
        ],
        out_specs=pl.BlockSpec((tm, tn), lambda i, j: (i, j)),
        out_shape=jax.ShapeDtypeStruct((m, n), out_dtype),
        scratch_shapes=[pltpu.VMEM((tm, d), BF16)],
        compiler_params=_params(("parallel", "arbitrary")),
        name="norm_mm",
    )(x, gain_all, w)


def _in_proj_kernel(x_ref, g_ref, w_ref, zr_ref, zf_ref, u_ref, *, f_blk):
    j = pl.program_id(1)

    @pl.when(j == 0)
    def _():
        u_ref[...] = _rmsnorm(x_ref[...], g_ref[...]).astype(BF16)

    z = _dot(u_ref[...], w_ref[...])

    @pl.when(j == f_blk)
    def _():
        zf_ref[...] = z

    @pl.when(j != f_blk)
    def _():
        zr_ref[...] = z.astype(BF16)


def _in_proj(x, gain_all, w, layer, tm, tn, f_lo):
    m, d = x.shape
    n = w.shape[-1]
    f_blk = f_lo // tn
    return pl.pallas_call(
        functools.partial(_in_proj_kernel, f_blk=f_blk),
        grid=(m // tm, n // tn),
        in_specs=[
            pl.BlockSpec((tm, d), lambda i, j: (i, 0)),
            pl.BlockSpec((None, 1, d), lambda i, j: (layer, 0, 0)),
            pl.BlockSpec((d, tn), lambda i, j: (0, j)),
        ],
        out_specs=[
            pl.BlockSpec((tm, tn), lambda i, j: (i, j - (j >= f_blk).astype(jnp.int32))),
            pl.BlockSpec((tm, tn), lambda i, j: (i, 0)),
        ],
        out_shape=[jax.ShapeDtypeStruct((m, n - tn), BF16), jax.ShapeDtypeStruct((m, tn), F32)],
        scratch_shapes=[pltpu.VMEM((tm, d), BF16)],
        compiler_params=_params(("parallel", "arbitrary")),
        name="in_proj",
    )(x, gain_all, w)


def _hgrn_tables(c):
    nlev = int(np.log2(c))
    t = np.arange(c)[:, None]
    j = np.arange(c)[None, :]
    blocks = [(j <= t)]
    for l in range(1, nlev + 1):
        m = 2 ** (l - 1)
        r = (t // (2 * m)) * (2 * m) + m - 1
        right = (t % (2 * m)) >= m
        blocks.append(np.where(right, (j > r) & (j <= t), (j > t) & (j <= r)))
    blocks.append(j > t)
    mall = np.concatenate(blocks, axis=0).astype(np.float32)
    mall = np.concatenate([mall, mall], axis=1)
    x = t ^ j
    lev = np.zeros((c, c), np.int32)
    low = j < t
    lev[low] = np.floor(np.log2(x[low])).astype(np.int32) + 1
    lev[np.arange(c), np.arange(c)] = nlev + 1
    return mall, lev, nlev


def _silu(z):
    return z / (1.0 + jnp.exp(-z))


def _hgrn_kernel(lb_ref, q_ref, f_ref, i_ref, og_ref, gain_ref, mall_ref, lev_ref, *rest,
                 layer, nlev, heads, n_cast):
    cast_src = rest[:n_cast]
    o_ref = rest[n_cast]
    cast_dst = rest[n_cast + 1:2 * n_cast + 1]
    state_ref, lbs_ref = rest[2 * n_cast + 1:]
    c = HGRN_CHUNK
    hw = HGRN_HEAD
    pw = 2 * hw

    for src, dst in zip(cast_src, cast_dst):
        dst[...] = src[...].astype(BF16)

    @pl.when(pl.program_id(1) == 0)
    def _():
        state_ref[...] = jnp.zeros_like(state_ref)
        x = lb_ref[...]
        e = jnp.exp(x - jnp.max(x, axis=0, keepdims=True))
        sm = e / jnp.sum(e, axis=0, keepdims=True)
        lb = jnp.zeros_like(sm[0:1])
        for r in range(1, layer + 1):
            lb = lb + sm[r:r + 1]
        lbs_ref[0:1, :] = jnp.log2(lb)
        lbs_ref[1:2, :] = jnp.log2(1.0 - lb)
        lbs_ref[2:3, :] = 1.0 - lb

    sub = SUBLANES
    nblk = c // sub
    lev = lev_ref[...]
    lev_blk = [lev[i * sub:(i + 1) * sub] for i in range(nblk)]
    row_in_blk = lax.broadcasted_iota(jnp.int32, (sub, hw), 0)

    def is_right(row, m):
        return (row & m) != 0

    def level_operand(q_rows, k_rows, w_l, m):
        if m >= sub:
            qk = [q_rows[i] if is_right(i * sub, m) else k_rows[i] for i in range(nblk)]
        else:
            pick_q = is_right(row_in_blk, m)
            qk = [jnp.where(pick_q, q_rows[i], k_rows[i]) for i in range(nblk)]
        return (jnp.concatenate(qk, axis=0) * w_l).astype(BF16)

    def one_head(q_h, k_h, w_h, v, hidx):
        q_rows = [q_h[i * sub:(i + 1) * sub] for i in range(nblk)]
        k_rows = [k_h[i * sub:(i + 1) * sub] for i in range(nblk)]
        s0 = _dot_nt(q_h.astype(BF16), k_h.astype(BF16))
        s_rows = [jnp.where(lev_blk[i] == nlev + 1, s0[i * sub:(i + 1) * sub], 0.0) for i in range(nblk)]
        for l in range(1, nlev + 1):
            m = 2 ** (l - 1)
            x = level_operand(q_rows, k_rows, w_h[l * c:(l + 1) * c], m)
            s_l = _dot_nt(x, x)
            for i in range(nblk):
                if m < sub or is_right(i * sub, m):
                    s_rows[i] = jnp.where(lev_blk[i] == l, s_l[i * sub:(i + 1) * sub], s_rows[i])
        s = jnp.concatenate(s_rows, axis=0).astype(BF16)
        state_t = state_ref[hidx]
        o = _dot(s, v) + _dot_nt((q_h * w_h[0:c]).astype(BF16), state_t.astype(BF16))
        k_dec = (k_h * w_h[(nlev + 1) * c:(nlev + 2) * c]).astype(BF16)
        state_ref[hidx] = state_t * w_h[c - 1:c] + _dot_tn(v, k_dec)
        return o * lax.rsqrt(jnp.mean(o * o, axis=1, keepdims=True) + EPS)

    for hp in range(heads // 2):
        ps = slice(hp * pw, (hp + 1) * pw)
        q = _silu(q_ref[:, ps]).astype(F32)
        log2_lb = lbs_ref[0:1, ps]
        log2_1m_lb = lbs_ref[1:2, ps]
        one_m_lb = lbs_ref[2:3, ps]
        z2 = f_ref[:, ps] * LOG2E
        e = jnp.exp2(-jnp.abs(z2))
        ope = 1.0 + e
        rcp = 1.0 / ope
        k = one_m_lb * jnp.where(z2 >= 0, e * rcp, rcp)
        cc = log2_1m_lb + (jnp.minimum(z2, 0.0) - jnp.log2(ope))
        g = jnp.maximum(log2_lb, cc) + jnp.log2(1.0 + jnp.exp2(-jnp.abs(log2_lb - cc)))
        g_hi = g.astype(BF16)
        g_lo = (g - g_hi.astype(F32)).astype(BF16)
        w = jnp.exp2(_dot(mall_ref[...], jnp.concatenate([g_hi, g_lo], axis=0)))
        outs = []
        for j in range(2):
            cols = slice(j * hw, (j + 1) * hw)
            v = i_ref[:, hp * pw + j * hw:hp * pw + (j + 1) * hw]
            outs.append(one_head(q[:, cols], k[:, cols], w[:, cols], v, 2 * hp + j))
        o = (jnp.concatenate(outs, axis=1) * gain_ref[:, ps]).astype(BF16)
        o_ref[:, ps] = o * _silu(og_ref[:, ps])


def _cast_plan(w, layer, steps):
    _, r, cols = w.shape
    csplit = 1
    while r % ((steps // csplit) * BF16_ROW_TILE) != 0:
        csplit *= 2
    assert steps % csplit == 0 and cols % (csplit * LANES) == 0, (w.shape, steps)
    br, bc = r // (steps // csplit), cols // csplit
    in_spec = lambda step: pl.BlockSpec((None, br, bc), lambda b, t: (layer, step(b, t) // csplit, step(b, t) % csplit))
    out_spec = lambda step: pl.BlockSpec((br, bc), lambda b, t: (step(b, t) // csplit, step(b, t) % csplit))
    return in_spec, out_spec, jax.ShapeDtypeStruct((r, cols), BF16)


def _hgrn(zr, zf, lb_raw, gain_all, layer, batch, seq, mall, lev, nlev, casts):
    m = zr.shape[0]
    hd = zf.shape[1]
    depth = lb_raw.shape[0]
    c = HGRN_CHUNK
    nc = seq // c
    heads = hd // HGRN_HEAD
    row = lambda b, t: b * nc + t
    plans = [_cast_plan(w, wl, batch * nc) for w, wl in casts]
    return pl.pallas_call(
        functools.partial(_hgrn_kernel, layer=layer, nlev=nlev, heads=heads, n_cast=len(casts)),
        grid=(batch, nc),
        in_specs=[
            _resident((depth, hd), lambda b, t: (0, 0)),
            pl.BlockSpec((c, hd), lambda b, t: (row(b, t), 1)),
            pl.BlockSpec((c, hd), lambda b, t: (row(b, t), 0)),
            pl.BlockSpec((c, hd), lambda b, t: (row(b, t), 2)),
            pl.BlockSpec((c, hd), lambda b, t: (row(b, t), 3)),
            pl.BlockSpec((None, 1, hd), lambda b, t: (layer, 0, 0)),
            _resident(mall.shape, lambda b, t: (0, 0)),
            _resident(lev.shape, lambda b, t: (0, 0)),
        ] + [p[0](row) for p in plans],
        out_specs=[pl.BlockSpec((c, hd), lambda b, t: (row(b, t), 0))] + [p[1](row) for p in plans],
        out_shape=[jax.ShapeDtypeStruct((m, hd), BF16)] + [p[2] for p in plans],
        scratch_shapes=[pltpu.VMEM((heads, HGRN_HEAD, HGRN_HEAD), F32),
                        pltpu.VMEM((8, hd), F32)],
        compiler_params=_params(("parallel", "arbitrary")),
        name="hgrn",
    )(lb_raw, zr, zf, zr, zr, gain_all, mall, lev, *[w for w, _ in casts])


def _mix_kernel(zp_ref, halo_ref, ga_ref, gb_ref, bo_ref, h_ref, wg_ref, sc_ref,
                wbp_ref, wbh_ref, wmo_ref, o_ref, *, tm):
    t0 = pl.program_id(1) * tm
    x = zp_ref[...].astype(F32)
    halo = jnp.where(t0 > 0, halo_ref[...].astype(F32), 0.0)
    ext = jnp.concatenate([halo, x], axis=0)
    pos = lax.broadcasted_iota(jnp.int32, (tm, 1), 0) + t0
    gd = x.shape[1] // len(POOL_WINDOWS)
    outs = []
    for g, wdw in enumerate(POOL_WINDOWS):
        cols = slice(g * gd, (g + 1) * gd)
        e = ext[:, cols]
        sh = 1
        while sh < wdw:
            e = e + pltpu.roll(e, sh, axis=0)
            sh *= 2
        cnt = jnp.minimum(pos + 1, wdw).astype(F32)
        pooled = (e[POOL_HALO:] / cnt - x[:, cols]).astype(BF16)
        outs.append((_dot(pooled, wg_ref[g]) * sc_ref[:, cols]).astype(BF16))
    a_out = jnp.concatenate(outs, axis=1)
    merged = _sigmoid(ga_ref[...].astype(F32)) * _dot(a_out, wbp_ref[...])
    merged = merged + _sigmoid(gb_ref[...].astype(F32)) * _dot(bo_ref[...], wbh_ref[...])
    o_ref[...] = h_ref[...] + _dot(merged.astype(BF16), wmo_ref[...])


def _mix(zr, b_out, h, wg_all, sc_all, wbp, wbh, wmo, layer, batch, seq, tm):
    m, d = h.shape
    pd = b_out.shape[1]
    ng, gd = wg_all.shape[1], wg_all.shape[2]
    nt = seq // tm
    row = lambda b, t: b * nt + t
    hb = tm // POOL_HALO
    return pl.pallas_call(
        functools.partial(_mix_kernel, tm=tm),
        grid=(batch, nt),
        in_specs=[
            pl.BlockSpec((tm, pd), lambda b, t: (row(b, t), 0)),
            pl.BlockSpec((POOL_HALO, pd), lambda b, t: (jnp.maximum(row(b, t) * hb - 1, 0), 0)),
            pl.BlockSpec((tm, d), lambda b, t: (row(b, t), 2)),
            pl.BlockSpec((tm, d), lambda b, t: (row(b, t), 3)),
            pl.BlockSpec((tm, pd), lambda b, t: (row(b, t), 0)),
            pl.BlockSpec((tm, d), lambda b, t: (row(b, t), 0)),
            _resident((None, ng, gd, gd), lambda b, t: (layer, 0, 0, 0)),
            _resident((None, 1, pd), lambda b, t: (layer, 0, 0)),
            _resident((pd, d), lambda b, t: (0, 0)),
            _resident((pd, d), lambda b, t: (0, 0)),
            _resident((d, d), lambda b, t: (0, 0)),
        ],
        out_specs=pl.BlockSpec((tm, d), lambda b, t: (row(b, t), 0)),
        out_shape=jax.ShapeDtypeStruct((m, d), F32),
        compiler_params=_params(("parallel", "parallel")),
        name="mix",
    )(zr, zr, zr, zr, b_out, h, wg_all, sc_all, wbp, wbh, wmo)


def _xattn_kernel(h_ref, g_ref, wq_ref, kv_ref, wo_ref, o_ref, att_ref):
    h = h_ref[...]
    d = h.shape[1]
    dh = d // X_HEADS
    u = _rmsnorm(h, g_ref[...]).astype(BF16)
    q = _dot(u, wq_ref[...]).astype(BF16)
    for hh in range(X_HEADS):
        cols = slice(hh * dh, (hh + 1) * dh)
        s = _dot_nt(q[:, cols], kv_ref[:, cols]) * (dh ** -0.5)
        p = jnp.exp(s - jnp.max(s, axis=-1, keepdims=True))
        p = p / jnp.sum(p, axis=-1, keepdims=True)
        att_ref[:, cols] = _dot(p.astype(BF16), kv_ref[:, d + hh * dh:d + (hh + 1) * dh]).astype(BF16)
    o_ref[...] = h + _dot(att_ref[...], wo_ref[...])


def _xattn(h, gain_all, wq, kv, wo, layer, batch, seq, mem_len, tm):
    m, d = h.shape
    nt = seq // tm
    row = lambda b, t: b * nt + t
    return pl.pallas_call(
        _xattn_kernel,
        grid=(batch, nt),
        in_specs=[
            pl.BlockSpec((tm, d), lambda b, t: (row(b, t), 0)),
            pl.BlockSpec((None, 1, d), lambda b, t: (layer, 0, 0)),
            _resident((d, d), lambda b, t: (0, 0)),
            pl.BlockSpec((mem_len, 2 * d), lambda b, t: (b, 0)),
            _resident((d, d), lambda b, t: (0, 0)),
        ],
        out_specs=pl.BlockSpec((tm, d), lambda b, t: (row(b, t), 0)),
        out_shape=jax.ShapeDtypeStruct((m, d), F32),
        scratch_shapes=[pltpu.VMEM((tm, d), BF16)],
        compiler_params=_params(("parallel", "parallel")),
        name="xattn",
    )(h, gain_all, wq, kv, wo)


def _ffn_kernel(h_ref, g_ref, wg_ref, wu_ref, wo_ref, *rest, final):
    if final:
        gf_ref, o_ref, u_ref = rest
    else:
        o_ref, u_ref = rest
    f = pl.program_id(1)

    @pl.when(f == 0)
    def _():
        h = h_ref[...]
        u_ref[...] = _rmsnorm(h, g_ref[...]).astype(BF16)
        o_ref[...] = h

    u = u_ref[...]
    gate = _dot(u, wg_ref[...])
    up = _dot(u, wu_ref[...])
    act = (gate * _sigmoid(gate) * up).astype(BF16)
    o_ref[...] += _dot(act, wo_ref[...])

    if final:
        @pl.when(f == pl.num_programs(1) - 1)
        def _():
            o_ref[...] = _rmsnorm(o_ref[...], gf_ref[...])


def _ffn(h, gain_all, w_in, w_out, layer, tm, tf, final_gain=None):
    m, d = h.shape
    dff = w_out.shape[0]
    nf = dff // tf
    final = final_gain is not None
    in_specs = [
        pl.BlockSpec((tm, d), lambda i, f: (i, 0)),
        pl.BlockSpec((None, 1, d), lambda i, f: (layer, 0, 0)),
        pl.BlockSpec((d, tf), lambda i, f: (0, f)),
        pl.BlockSpec((d, tf), lambda i, f: (0, nf + f)),
        pl.BlockSpec((tf, d), lambda i, f: (f, 0)),
    ]
    args = [h, gain_all, w_in, w_in, w_out]
    if final:
        in_specs.append(pl.BlockSpec((1, d), lambda i, f: (0, 0)))
        args.append(final_gain)
    return pl.pallas_call(
        functools.partial(_ffn_kernel, final=final),
        grid=(m // tm, nf),
        in_specs=in_specs,
        out_specs=pl.BlockSpec((tm, d), lambda i, f: (i, 0)),
        out_shape=jax.ShapeDtypeStruct((m, d), F32),
        scratch_shapes=[pltpu.VMEM((tm, d), BF16)],
        compiler_params=_params(("parallel", "arbitrary")),
        name="ffn",
    )(*args)


def _tiles(tokens, seq, mem_tokens, hd):
    return {
        "proj_rows": min(1024, tokens),
        "mix_rows": min(256, seq),
        "kv_rows": min(1024, mem_tokens),
        "kv_cols": hd,
        "xattn_rows": min(512, seq),
        "ffn_rows": min(1024, tokens),
        "ffn_cols": 512,
    }


def kernel(x, mem, w_in, w_pool_group, pool_scale, hgrn_lower_bounds, hgrn_norm, w_branch_pool,
           w_branch_hgrn, w_mix_out, norm_mix, norm_mem, norm_cross, w_xq, w_xkv, w_xo, norm_ffn,
           w_ffn_in, w_ffn_out, norm_final):
    batch, seq, d = x.shape
    mem_len = mem.shape[1]
    depth = w_in.shape[0]
    pd = w_branch_pool.shape[1]
    hd = hgrn_lower_bounds.shape[1]

    f_lo = pd + hd
    win = w_in[0].astype(BF16)
    wg = w_pool_group.astype(BF16)
    layer_weights = (w_branch_pool, w_branch_hgrn, w_mix_out, w_xq, w_xkv, w_xo, w_ffn_in, w_ffn_out)

    as_rows = lambda a: a.reshape(a.shape[0], 1, a.shape[1])
    g_mix, g_mem, g_cross, g_ffn = map(as_rows, (norm_mix, norm_mem, norm_cross, norm_ffn))
    g_hgrn, sc_pool = as_rows(hgrn_norm), as_rows(pool_scale)
    g_final = norm_final.reshape(1, d)

    mall_np, lev_np, nlev = _hgrn_tables(HGRN_CHUNK)
    mall = jnp.asarray(mall_np, BF16)
    lev = jnp.asarray(lev_np)

    t = _tiles(batch * seq, seq, batch * mem_len, hd)
    h = x.reshape(batch * seq, d)
    mem2 = mem.reshape(batch * mem_len, d)
    for l in range(depth):
        zr, zf = _in_proj(h, g_mix, win, l, t["proj_rows"], hd, f_lo)
        casts = [(w, l) for w in layer_weights] + ([(w_in, l + 1)] if l + 1 < depth else [])
        b_out, wbp, wbh, wmo, wxq, wxkv, wxo, wfi, wfo, *nxt = _hgrn(
            zr, zf, hgrn_lower_bounds, g_hgrn, l, batch, seq, mall, lev, nlev, casts)
        win = nxt[0] if nxt else None
        h = _mix(zr, b_out, h, wg, sc_pool, wbp, wbh, wmo, l, batch, seq, t["mix_rows"])
        kv = _norm_mm(mem2, g_mem, wxkv, l, BF16, t["kv_rows"], t["kv_cols"])
        h = _xattn(h, g_cross, wxq, kv, wxo, l, batch, seq, mem_len, t["xattn_rows"])
        h = _ffn(h, g_ffn, wfi, wfo, l, t["ffn_rows"], t["ffn_cols"], g_final if l == depth - 1 else None)
    return h.reshape(batch, seq, d)
```

```python
import functools

import numpy as np
import jax
import jax.numpy as jnp
from jax import lax
from jax.experimental import pallas as pl
from jax.experimental.pallas import tpu as pltpu

F32 = jnp.float32
BF16 = jnp.bfloat16
EPS = 1e-6
LOG2E = 1.4426950408889634

SUBLANES = 8
LANES = 128
BF16_ROW_TILE = 16

POOL_WINDOWS = (2, 4, 8, 16)
POOL_HALO = 16
MIX_SLABS = 2
HGRN_HEAD = 128
HGRN_CHUNK = 128
X_HEADS = 4

VMEM_LIMIT = 56 * 1024 * 1024


def _params(sem):
    return pltpu.CompilerParams(dimension_semantics=sem, vmem_limit_bytes=VMEM_LIMIT)


def _resident(block_shape, index_map):
    return pl.BlockSpec(block_shape, index_map, pipeline_mode=pl.Buffered(1))


def _rmsnorm(x, gain):
    ms = jnp.mean(x * x, axis=-1, keepdims=True)
    return x * lax.rsqrt(ms + EPS) * gain


def _dot(a, b):
    return jnp.dot(a, b, preferred_element_type=F32)


def _dot_nt(a, b):
    return lax.dot_general(a, b, (((1,), (1,)), ((), ())), preferred_element_type=F32)


def _dot_tn(a, b):
    return lax.dot_general(a, b, (((0,), (0,)), ((), ())), preferred_element_type=F32)


def _sigmoid(x):
    return 1.0 / (1.0 + jnp.exp(-x))


def _norm_mm_kernel(x_ref, g_ref, w_ref, o_ref, u_ref):
    @pl.when(pl.program_id(1) == 0)
    def _():
        u_ref[...] = _rmsnorm(x_ref[...], g_ref[...]).astype(BF16)

    o_ref[...] = _dot(u_ref[...], w_ref[...]).astype(o_ref.dtype)


def _norm_mm(x, gain_all, w, layer, out_dtype, tm, tn):
    m, d = x.shape
    n = w.shape[-1]
    return pl.pallas_call(
        _norm_mm_kernel,
        grid=(m // tm, n // tn),
        in_specs=[
            pl.BlockSpec((tm, d), lambda i, j: (i, 0)),
            pl.BlockSpec((None, 1, d), lambda i, j: (layer, 0, 0)),
            pl.BlockSpec((d, tn), lambda i, j: (0, j)),
        ],
        out_specs=pl.BlockSpec((tm, tn), lambda i, j: (i, j)),
        out_shape=jax.ShapeDtypeStruct((m, n), out_dtype),
        scratch_shapes=[pltpu.VMEM((tm, d), BF16)],
        compiler_params=_params(("parallel", "arbitrary")),
        name="norm_mm",
    )(x, gain_all, w)


def _in_proj_kernel(x_ref, g_ref, w_ref, zr_ref, zf_ref, u_ref):
    @pl.when(pl.program_id(1) == 0)
    def _():
        u_ref[...] = _rmsnorm(x_ref[...], g_ref[...]).astype(BF16)

    z = _dot(u_ref[...], w_ref[...])
    zr_ref[...] = z.astype(BF16)
    zf_ref[...] = z


def _in_proj(x, gain_all, w, layer, tm, tn, f_lo):
    m, d = x.shape
    n = w.shape[-1]
    nb = n // tn
    f_blk = f_lo // tn

    def w_col(j):
        return jnp.where(j == nb - 1, f_blk, j + (j >= f_blk).astype(jnp.int32))

    return pl.pallas_call(
        _in_proj_kernel,
        grid=(m // tm, nb),
        in_specs=[
            pl.BlockSpec((tm, d), lambda i, j: (i, 0)),
            pl.BlockSpec((None, 1, d), lambda i, j: (layer, 0, 0)),
            pl.BlockSpec((d, tn), lambda i, j: (0, w_col(j))),
        ],
        out_specs=[
            pl.BlockSpec((tm, tn), lambda i, j: (i, j)),
            pl.BlockSpec((tm, tn), lambda i, j: (i, 0)),
        ],
        out_shape=[jax.ShapeDtypeStruct((m, n), BF16), jax.ShapeDtypeStruct((m, tn), F32)],
        scratch_shapes=[pltpu.VMEM((tm, d), BF16)],
        compiler_params=_params(("parallel", "arbitrary")),
        name="in_proj",
    )(x, gain_all, w)


def _hgrn_tables(c):
    nlev = int(np.log2(c))
    t = np.arange(c)[:, None]
    j = np.arange(c)[None, :]
    blocks = [(j <= t)]
    for l in range(1, nlev + 1):
        m = 2 ** (l - 1)
        r = (t // (2 * m)) * (2 * m) + m - 1
        right = (t % (2 * m)) >= m
        blocks.append(np.where(right, (j > r) & (j <= t), (j > t) & (j <= r)))
    blocks.append(j > t)
    mall = np.concatenate(blocks, axis=0).astype(np.float32)
    mall = np.concatenate([mall, mall], axis=1)
    x = t ^ j
    lev = np.zeros((c, c), np.int32)
    low = j < t
    lev[low] = np.floor(np.log2(x[low])).astype(np.int32) + 1
    lev[np.arange(c), np.arange(c)] = nlev + 1
    return mall, lev, nlev


def _silu(z):
    return z / (1.0 + jnp.exp(-z))


def _hgrn_kernel(lb_ref, q_ref, f_ref, i_ref, og_ref, gain_ref, mall_ref, lev_ref, *rest,
                 layer, nlev, heads, n_cast):
    cast_src = rest[:n_cast]
    o_ref = rest[n_cast]
    cast_dst = rest[n_cast + 1:2 * n_cast + 1]
    state_ref, lbs_ref = rest[2 * n_cast + 1:]
    c = HGRN_CHUNK
    hw = HGRN_HEAD
    pw = 2 * hw

    for src, dst in zip(cast_src, cast_dst):
        dst[...] = src[...].astype(BF16)

    @pl.when(pl.program_id(1) == 0)
    def _():
        state_ref[...] = jnp.zeros_like(state_ref)
        x = lb_ref[...]
        e = jnp.exp(x - jnp.max(x, axis=0, keepdims=True))
        sm = e / jnp.sum(e, axis=0, keepdims=True)
        lb = jnp.zeros_like(sm[0:1])
        for r in range(1, layer + 1):
            lb = lb + sm[r:r + 1]
        lbs_ref[0:1, :] = jnp.log2(lb)
        lbs_ref[1:2, :] = jnp.log2(1.0 - lb)
        lbs_ref[2:3, :] = 1.0 - lb

    sub = SUBLANES
    nblk = c // sub
    lev = lev_ref[...]
    lev_blk = [lev[i * sub:(i + 1) * sub] for i in range(nblk)]
    row_in_blk = lax.broadcasted_iota(jnp.int32, (sub, hw), 0)

    def is_right(row, m):
        return (row & m) != 0

    def level_operand(q_rows, k_rows, w_l, m):
        if m >= sub:
            qk = [q_rows[i] if is_right(i * sub, m) else k_rows[i] for i in range(nblk)]
        else:
            pick_q = is_right(row_in_blk, m)
            qk = [jnp.where(pick_q, q_rows[i], k_rows[i]) for i in range(nblk)]
        return (jnp.concatenate(qk, axis=0) * w_l).astype(BF16)

    def one_head(q_h, k_h, w_h, v, hidx):
        q_rows = [q_h[i * sub:(i + 1) * sub] for i in range(nblk)]
        k_rows = [k_h[i * sub:(i + 1) * sub] for i in range(nblk)]
        s0 = _dot_nt(q_h.astype(BF16), k_h.astype(BF16))
        s_rows = [jnp.where(lev_blk[i] == nlev + 1, s0[i * sub:(i + 1) * sub], 0.0) for i in range(nblk)]
        for l in range(1, nlev + 1):
            m = 2 ** (l - 1)
            x = level_operand(q_rows, k_rows, w_h[l * c:(l + 1) * c], m)
            s_l = _dot_nt(x, x)
            for i in range(nblk):
                if m < sub or is_right(i * sub, m):
                    s_rows[i] = jnp.where(lev_blk[i] == l, s_l[i * sub:(i + 1) * sub], s_rows[i])
        s = jnp.concatenate(s_rows, axis=0).astype(BF16)
        state_t = state_ref[hidx]
        o = _dot(s, v) + _dot_nt((q_h * w_h[0:c]).astype(BF16), state_t.astype(BF16))
        k_dec = (k_h * w_h[(nlev + 1) * c:(nlev + 2) * c]).astype(BF16)
        state_ref[hidx] = state_t * w_h[c - 1:c] + _dot_tn(v, k_dec)
        return o * lax.rsqrt(jnp.mean(o * o, axis=1, keepdims=True) + EPS)

    for hp in range(heads // 2):
        ps = slice(hp * pw, (hp + 1) * pw)
        q = _silu(q_ref[:, ps]).astype(F32)
        log2_lb = lbs_ref[0:1, ps]
        log2_1m_lb = lbs_ref[1:2, ps]
        one_m_lb = lbs_ref[2:3, ps]
        z2 = f_ref[:, ps] * LOG2E
        e = jnp.exp2(-jnp.abs(z2))
        ope = 1.0 + e
        rcp = 1.0 / ope
        k = one_m_lb * jnp.where(z2 >= 0, e * rcp, rcp)
        cc = log2_1m_lb + (jnp.minimum(z2, 0.0) - jnp.log2(ope))
        g = jnp.maximum(log2_lb, cc) + jnp.log2(1.0 + jnp.exp2(-jnp.abs(log2_lb - cc)))
        g_hi = g.astype(BF16)
        g_lo = (g - g_hi.astype(F32)).astype(BF16)
        w = jnp.exp2(_dot(mall_ref[...], jnp.concatenate([g_hi, g_lo], axis=0)))
        outs = []
        for j in range(2):
            cols = slice(j * hw, (j + 1) * hw)
            v = i_ref[:, hp * pw + j * hw:hp * pw + (j + 1) * hw]
            outs.append(one_head(q[:, cols], k[:, cols], w[:, cols], v, 2 * hp + j))
        o = (jnp.concatenate(outs, axis=1) * gain_ref[:, ps]).astype(BF16)
        o_ref[:, ps] = o * _silu(og_ref[:, ps])


def _cast_plan(w, layer, steps):
    _, r, cols = w.shape
    csplit = 1
    while r % ((steps // csplit) * BF16_ROW_TILE) != 0:
        csplit *= 2
    assert steps % csplit == 0 and cols % (csplit * LANES) == 0, (w.shape, steps)
    br, bc = r // (steps // csplit), cols // csplit
    in_spec = lambda step: pl.BlockSpec((None, br, bc), lambda b, t: (layer, step(b, t) // csplit, step(b, t) % csplit))
    out_spec = lambda step: pl.BlockSpec((br, bc), lambda b, t: (step(b, t) // csplit, step(b, t) % csplit))
    return in_spec, out_spec, jax.ShapeDtypeStruct((r, cols), BF16)


def _hgrn(zr, zf, lb_raw, gain_all, layer, batch, seq, mall, lev, nlev, casts):
    m = zr.shape[0]
    hd = zf.shape[1]
    depth = lb_raw.shape[0]
    c = HGRN_CHUNK
    nc = seq // c
    heads = hd // HGRN_HEAD
    row = lambda b, t: b * nc + t
    plans = [_cast_plan(w, wl, batch * nc) for w, wl in casts]
    return pl.pallas_call(
        functools.partial(_hgrn_kernel, layer=layer, nlev=nlev, heads=heads, n_cast=len(casts)),
        grid=(batch, nc),
        in_specs=[
            _resident((depth, hd), lambda b, t: (0, 0)),
            pl.BlockSpec((c, hd), lambda b, t: (row(b, t), 1)),
            pl.BlockSpec((c, hd), lambda b, t: (row(b, t), 0)),
            pl.BlockSpec((c, hd), lambda b, t: (row(b, t), 2)),
            pl.BlockSpec((c, hd), lambda b, t: (row(b, t), 3)),
            pl.BlockSpec((None, 1, hd), lambda b, t: (layer, 0, 0)),
            _resident(mall.shape, lambda b, t: (0, 0)),
            _resident(lev.shape, lambda b, t: (0, 0)),
        ] + [p[0](row) for p in plans],
        out_specs=[pl.BlockSpec((c, hd), lambda b, t: (row(b, t), 0))] + [p[1](row) for p in plans],
        out_shape=[jax.ShapeDtypeStruct((m, hd), BF16)] + [p[2] for p in plans],
        scratch_shapes=[pltpu.VMEM((heads, HGRN_HEAD, HGRN_HEAD), F32),
                        pltpu.VMEM((8, hd), F32)],
        compiler_params=_params(("parallel", "arbitrary")),
        name="hgrn",
    )(lb_raw, zr, zf, zr, zr, gain_all, mall, lev, *[w for w, _ in casts])


def _mix_kernel(zp_ref, halo_ref, ga_ref, gb_ref, bo_ref, h_ref, wg_ref, sc_ref,
                wbp_ref, wbh_ref, wmo_ref, o_ref, *, tm):
    rows = tm // MIX_SLABS
    gd = zp_ref.shape[1] // len(POOL_WINDOWS)
    for part in range(MIX_SLABS):
        rs = slice(part * rows, (part + 1) * rows)
        t0 = pl.program_id(1) * tm + part * rows
        x = zp_ref[rs, :].astype(F32)
        if part == 0:
            halo = jnp.where(t0 > 0, halo_ref[...].astype(F32), 0.0)
        else:
            halo = zp_ref[part * rows - POOL_HALO:part * rows, :].astype(F32)
        ext = jnp.concatenate([halo, x], axis=0)
        pos = lax.broadcasted_iota(jnp.int32, (rows, 1), 0) + t0
        outs = []
        for g, wdw in enumerate(POOL_WINDOWS):
            cols = slice(g * gd, (g + 1) * gd)
            e = ext[:, cols]
            sh = 1
            while sh < wdw:
                e = e + pltpu.roll(e, sh, axis=0)
                sh *= 2
            cnt = jnp.minimum(pos + 1, wdw).astype(F32)
            pooled = (e[POOL_HALO:] / cnt - x[:, cols]).astype(BF16)
            outs.append((_dot(pooled, wg_ref[g]) * sc_ref[:, cols]).astype(BF16))
        a_out = jnp.concatenate(outs, axis=1)
        merged = _sigmoid(ga_ref[rs, :].astype(F32)) * _dot(a_out, wbp_ref[...])
        merged = merged + _sigmoid(gb_ref[rs, :].astype(F32)) * _dot(bo_ref[rs, :], wbh_ref[...])
        o_ref[rs, :] = h_ref[rs, :] + _dot(merged.astype(BF16), wmo_ref[...])


def _mix(zr, b_out, h, wg_all, sc_all, wbp, wbh, wmo, layer, batch, seq, tm):
    m, d = h.shape
    pd = b_out.shape[1]
    ng, gd = wg_all.shape[1], wg_all.shape[2]
    nt = seq // tm
    row = lambda b, t: b * nt + t
    hb = tm // POOL_HALO
    return pl.pallas_call(
        functools.partial(_mix_kernel, tm=tm),
        grid=(batch, nt),
        in_specs=[
            pl.BlockSpec((tm, pd), lambda b, t: (row(b, t), 0)),
            pl.BlockSpec((POOL_HALO, pd), lambda b, t: (jnp.maximum(row(b, t) * hb - 1, 0), 0)),
            pl.BlockSpec((tm, d), lambda b, t: (row(b, t), 2)),
            pl.BlockSpec((tm, d), lambda b, t: (row(b, t), 3)),
            pl.BlockSpec((tm, pd), lambda b, t: (row(b, t), 0)),
            pl.BlockSpec((tm, d), lambda b, t: (row(b, t), 0)),
            _resident((None, ng, gd, gd), lambda b, t: (layer, 0, 0, 0)),
            _resident((None, 1, pd), lambda b, t: (layer, 0, 0)),
            _resident((pd, d), lambda b, t: (0, 0)),
            _resident((pd, d), lambda b, t: (0, 0)),
            _resident((d, d), lambda b, t: (0, 0)),
        ],
        out_specs=pl.BlockSpec((tm, d), lambda b, t: (row(b, t), 0)),
        out_shape=jax.ShapeDtypeStruct((m, d), F32),
        compiler_params=_params(("parallel", "parallel")),
        name="mix",
    )(zr, zr, zr, zr, b_out, h, wg_all, sc_all, wbp, wbh, wmo)


def _xattn_kernel(h_ref, g_ref, wq_ref, kv_ref, wo_ref, o_ref, att_ref):
    h = h_ref[...]
    d = h.shape[1]
    dh = d // X_HEADS
    u = _rmsnorm(h, g_ref[...]).astype(BF16)
    q = _dot(u, wq_ref[...]).astype(BF16)
    for hh in range(X_HEADS):
        cols = slice(hh * dh, (hh + 1) * dh)
        s = _dot_nt(q[:, cols], kv_ref[:, cols]) * (dh ** -0.5)
        p = jnp.exp(s - jnp.max(s, axis=-1, keepdims=True))
        p = p / jnp.sum(p, axis=-1, keepdims=True)
        att_ref[:, cols] = _dot(p.astype(BF16), kv_ref[:, d + hh * dh:d + (hh + 1) * dh]).astype(BF16)
    o_ref[...] = h + _dot(att_ref[...], wo_ref[...])


def _xattn(h, gain_all, wq, kv, wo, layer, batch, seq, mem_len, tm):
    m, d = h.shape
    nt = seq // tm
    row = lambda b, t: b * nt + t
    return pl.pallas_call(
        _xattn_kernel,
        grid=(batch, nt),
        in_specs=[
            pl.BlockSpec((tm, d), lambda b, t: (row(b, t), 0)),
            pl.BlockSpec((None, 1, d), lambda b, t: (layer, 0, 0)),
            _resident((d, d), lambda b, t: (0, 0)),
            pl.BlockSpec((mem_len, 2 * d), lambda b, t: (b, 0)),
            _resident((d, d), lambda b, t: (0, 0)),
        ],
        out_specs=pl.BlockSpec((tm, d), lambda b, t: (row(b, t), 0)),
        out_shape=jax.ShapeDtypeStruct((m, d), F32),
        scratch_shapes=[pltpu.VMEM((tm, d), BF16)],
        compiler_params=_params(("parallel", "parallel")),
        name="xattn",
    )(h, gain_all, wq, kv, wo)


def _ffn_kernel(h_ref, g_ref, wg_ref, wu_ref, wo_ref, *rest, final):
    if final:
        gf_ref, o_ref, u_ref = rest
    else:
        o_ref, u_ref = rest
    f = pl.program_id(1)

    @pl.when(f == 0)
    def _():
        h = h_ref[...]
        u_ref[...] = _rmsnorm(h, g_ref[...]).astype(BF16)
        o_ref[...] = h

    u = u_ref[...]
    gate = _dot(u, wg_ref[...])
    up = _dot(u, wu_ref[...])
    act = (gate * _sigmoid(gate) * up).astype(BF16)
    o_ref[...] += _dot(act, wo_ref[...])

    if final:
        @pl.when(f == pl.num_programs(1) - 1)
        def _():
            o_ref[...] = _rmsnorm(o_ref[...], gf_ref[...])


def _ffn(h, gain_all, w_in, w_out, layer, tm, tf, final_gain=None):
    m, d = h.shape
    dff = w_out.shape[0]
    nf = dff // tf
    final = final_gain is not None
    in_specs = [
        pl.BlockSpec((tm, d), lambda i, f: (i, 0)),
        pl.BlockSpec((None, 1, d), lambda i, f: (layer, 0, 0)),
        pl.BlockSpec((d, tf), lambda i, f: (0, f)),
        pl.BlockSpec((d, tf), lambda i, f: (0, nf + f)),
        pl.BlockSpec((tf, d), lambda i, f: (f, 0)),
    ]
    args = [h, gain_all, w_in, w_in, w_out]
    if final:
        in_specs.append(pl.BlockSpec((1, d), lambda i, f: (0, 0)))
        args.append(final_gain)
    return pl.pallas_call(
        functools.partial(_ffn_kernel, final=final),
        grid=(m // tm, nf),
        in_specs=in_specs,
        out_specs=pl.BlockSpec((tm, d), lambda i, f: (i, 0)),
        out_shape=jax.ShapeDtypeStruct((m, d), F32),
        scratch_shapes=[pltpu.VMEM((tm, d), BF16)],
        compiler_params=_params(("parallel", "arbitrary")),
        name="ffn",
    )(*args)


def _tiles(tokens, seq, mem_tokens, hd):
    return {
        "proj_rows": min(1024, tokens),
        "mix_rows": min(512, seq),
        "kv_rows": min(1024, mem_tokens),
        "kv_cols": hd,
        "xattn_rows": min(512, seq),
        "ffn_rows": min(1024, tokens),
        "ffn_cols": 512,
    }


def kernel(x, mem, w_in, w_pool_group, pool_scale, hgrn_lower_bounds, hgrn_norm, w_branch_pool,
           w_branch_hgrn, w_mix_out, norm_mix, norm_mem, norm_cross, w_xq, w_xkv, w_xo, norm_ffn,
           w_ffn_in, w_ffn_out, norm_final):
    batch, seq, d = x.shape
    mem_len = mem.shape[1]
    depth = w_in.shape[0]
    pd = w_branch_pool.shape[1]
    hd = hgrn_lower_bounds.shape[1]

    f_lo = pd + hd
    win = w_in[0].astype(BF16)
    wg = w_pool_group.astype(BF16)
    layer_weights = (w_branch_pool, w_branch_hgrn, w_mix_out, w_xq, w_xkv, w_xo, w_ffn_in, w_ffn_out)

    as_rows = lambda a: a.reshape(a.shape[0], 1, a.shape[1])
    g_mix, g_mem, g_cross, g_ffn = map(as_rows, (norm_mix, norm_mem, norm_cross, norm_ffn))
    g_hgrn, sc_pool = as_rows(hgrn_norm), as_rows(pool_scale)
    g_final = norm_final.reshape(1, d)

    mall_np, lev_np, nlev = _hgrn_tables(HGRN_CHUNK)
    mall = jnp.asarray(mall_np, BF16)
    lev = jnp.asarray(lev_np)

    t = _tiles(batch * seq, seq, batch * mem_len, hd)
    h = x.reshape(batch * seq, d)
    mem2 = mem.reshape(batch * mem_len, d)
    for l in range(depth):
        zr, zf = _in_proj(h, g_mix, win, l, t["proj_rows"], hd, f_lo)
        casts = [(w, l) for w in layer_weights] + ([(w_in, l + 1)] if l + 1 < depth else [])
        b_out, wbp, wbh, wmo, wxq, wxkv, wxo, wfi, wfo, *nxt = _hgrn(
            zr, zf, hgrn_lower_bounds, g_hgrn, l, batch, seq, mall, lev, nlev, casts)
        win = nxt[0] if nxt else None
        h = _mix(zr, b_out, h, wg, sc_pool, wbp, wbh, wmo, l, batch, seq, t["mix_rows"])
        kv = _norm_mm(mem2, g_mem, wxkv, l, BF16, t["kv_rows"], t["kv_cols"])
        h = _xattn(h, g_cross, wxq, kv, wxo, l, batch, seq, mem_len, t["xattn_rows"])
        h = _ffn(h, g_ffn, wfi, wfo, l, t["ffn_rows"], t["ffn_cols"], g_final if l == depth - 1 else None)
    return h.reshape(batch, seq, d)
```

```python
import functools

import numpy as np
import jax
import jax.numpy as jnp
from jax import lax
from jax.experimental import pallas as pl
from jax.experimental.pallas import tpu as pltpu

F32 = jnp.float32
BF16 = jnp.bfloat16
EPS = 1e-6
LOG2E = 1.4426950408889634

SUBLANES = 8
LANES = 128
BF16_ROW_TILE = 16

POOL_WINDOWS = (2, 4, 8, 16)
POOL_HALO = 16
MIX_SLABS = 2
HGRN_HEAD = 128
HGRN_CHUNK = 128
X_HEADS = 4

VMEM_LIMIT = 56 * 1024 * 1024


def _params(sem):
    return pltpu.CompilerParams(dimension_semantics=sem, vmem_limit_bytes=VMEM_LIMIT)


def _resident(block_shape, index_map):
    return pl.BlockSpec(block_shape, index_map, pipeline_mode=pl.Buffered(1))


def _rmsnorm(x, gain):
    ms = jnp.mean(x * x, axis=-1, keepdims=True)
    return x * lax.rsqrt(ms + EPS) * gain


def _dot(a, b):
    return jnp.dot(a, b, preferred_element_type=F32)


def _dot_nt(a, b):
    return lax.dot_general(a, b, (((1,), (1,)), ((), ())), preferred_element_type=F32)


def _dot_tn(a, b):
    return lax.dot_general(a, b, (((0,), (0,)), ((), ())), preferred_element_type=F32)


def _sigmoid(x):
    return 1.0 / (1.0 + jnp.exp(-x))


def _norm_mm_kernel(x_ref, g_ref, w_ref, o_ref, u_ref):
    @pl.when(pl.program_id(1) == 0)
    def _():
        u_ref[...] = _rmsnorm(x_ref[...], g_ref[...]).astype(BF16)

    o_ref[...] = _dot(u_ref[...], w_ref[...]).astype(o_ref.dtype)


def _norm_mm(x, gain_all, w, layer, out_dtype, tm, tn):
    m, d = x.shape
    n = w.shape[-1]
    return pl.pallas_call(
        _norm_mm_kernel,
        grid=(m // tm, n // tn),
        in_specs=[
            pl.BlockSpec((tm, d), lambda i, j: (i, 0)),
            pl.BlockSpec((None, 1, d), lambda i, j: (layer, 0, 0)),
            pl.BlockSpec((d, tn), lambda i, j: (0, j)),
        ],
        out_specs=pl.BlockSpec((tm, tn), lambda i, j: (i, j)),
        out_shape=jax.ShapeDtypeStruct((m, n), out_dtype),
        scratch_shapes=[pltpu.VMEM((tm, d), BF16)],
        compiler_params=_params(("parallel", "arbitrary")),
        name="norm_mm",
    )(x, gain_all, w)


def _hgrn_tables(c):
    nlev = int(np.log2(c))
    t = np.arange(c)[:, None]
    j = np.arange(c)[None, :]
    blocks = [(j <= t)]
    for l in range(1, nlev + 1):
        m = 2 ** (l - 1)
        r = (t // (2 * m)) * (2 * m) + m - 1
        right = (t % (2 * m)) >= m
        blocks.append(np.where(right, (j > r) & (j <= t), (j > t) & (j <= r)))
    blocks.append(j > t)
    mall = np.concatenate(blocks, axis=0).astype(np.float32)
    mall = np.concatenate([mall, mall], axis=1)
    x = t ^ j
    lev = np.zeros((c, c), np.int32)
    low = j < t
    lev[low] = np.floor(np.log2(x[low])).astype(np.int32) + 1
    lev[np.arange(c), np.arange(c)] = nlev + 1
    return mall, lev, nlev


def _silu(z):
    return z / (1.0 + jnp.exp(-z))


def _lower_bound_tables(lb_ref, lbs_ref, layer):
    x = lb_ref[...]
    e = jnp.exp(x - jnp.max(x, axis=0, keepdims=True))
    sm = e / jnp.sum(e, axis=0, keepdims=True)
    lb = jnp.zeros_like(sm[0:1])
    for r in range(1, layer + 1):
        lb = lb + sm[r:r + 1]
    lbs_ref[0:1, :] = jnp.log2(lb)
    lbs_ref[1:2, :] = jnp.log2(1.0 - lb)
    lbs_ref[2:3, :] = 1.0 - lb


def _hgrn_chunk(zs_ref, f_ref, rows, gain_ref, mall_ref, lev_ref, lbs_ref, state_ref, o_ref, nlev, heads,
                side_work):
    c = HGRN_CHUNK
    hw = HGRN_HEAD
    pw = 2 * hw
    q_ref, i_ref, og_ref = zs_ref.at[0], zs_ref.at[1], zs_ref.at[2]
    sub = SUBLANES
    nblk = c // sub
    lev = lev_ref[...]
    lev_blk = [lev[i * sub:(i + 1) * sub] for i in range(nblk)]
    row_in_blk = lax.broadcasted_iota(jnp.int32, (sub, hw), 0)

    def is_right(row, m):
        return (row & m) != 0

    def level_operand(q_rows, k_rows, w_l, m):
        if m >= sub:
            qk = [q_rows[i] if is_right(i * sub, m) else k_rows[i] for i in range(nblk)]
        else:
            pick_q = is_right(row_in_blk, m)
            qk = [jnp.where(pick_q, q_rows[i], k_rows[i]) for i in range(nblk)]
        return (jnp.concatenate(qk, axis=0) * w_l).astype(BF16)

    def one_head(q_h, k_h, w_h, v, hidx):
        q_rows = [q_h[i * sub:(i + 1) * sub] for i in range(nblk)]
        k_rows = [k_h[i * sub:(i + 1) * sub] for i in range(nblk)]
        s0 = _dot_nt(q_h.astype(BF16), k_h.astype(BF16))
        s_rows = [jnp.where(lev_blk[i] == nlev + 1, s0[i * sub:(i + 1) * sub], 0.0) for i in range(nblk)]
        for l in range(1, nlev + 1):
            m = 2 ** (l - 1)
            x = level_operand(q_rows, k_rows, w_h[l * c:(l + 1) * c], m)
            s_l = _dot_nt(x, x)
            for i in range(nblk):
                if m < sub or is_right(i * sub, m):
                    s_rows[i] = jnp.where(lev_blk[i] == l, s_l[i * sub:(i + 1) * sub], s_rows[i])
        s = jnp.concatenate(s_rows, axis=0).astype(BF16)
        state_t = state_ref[hidx]
        o = _dot(s, v) + _dot_nt((q_h * w_h[0:c]).astype(BF16), state_t.astype(BF16))
        k_dec = (k_h * w_h[(nlev + 1) * c:(nlev + 2) * c]).astype(BF16)
        state_ref[hidx] = state_t * w_h[c - 1:c] + _dot_tn(v, k_dec)
        return o * lax.rsqrt(jnp.mean(o * o, axis=1, keepdims=True) + EPS)

    for hp in range(heads // 2):
        side_work(hp, heads // 2)
        ps = slice(hp * pw, (hp + 1) * pw)
        q = _silu(q_ref[rows, ps]).astype(F32)
        log2_lb = lbs_ref[0:1, ps]
        log2_1m_lb = lbs_ref[1:2, ps]
        one_m_lb = lbs_ref[2:3, ps]
        z2 = f_ref[rows, ps] * LOG2E
        e = jnp.exp2(-jnp.abs(z2))
        ope = 1.0 + e
        rcp = 1.0 / ope
        k = one_m_lb * jnp.where(z2 >= 0, e * rcp, rcp)
        cc = log2_1m_lb + (jnp.minimum(z2, 0.0) - jnp.log2(ope))
        g = jnp.maximum(log2_lb, cc) + jnp.log2(1.0 + jnp.exp2(-jnp.abs(log2_lb - cc)))
        g_hi = g.astype(BF16)
        g_lo = (g - g_hi.astype(F32)).astype(BF16)
        w = jnp.exp2(_dot(mall_ref[...], jnp.concatenate([g_hi, g_lo], axis=0)))
        outs = []
        for j in range(2):
            cols = slice(j * hw, (j + 1) * hw)
            v = i_ref[rows, hp * pw + j * hw:hp * pw + (j + 1) * hw]
            outs.append(one_head(q[:, cols], k[:, cols], w[:, cols], v, 2 * hp + j))
        o = (jnp.concatenate(outs, axis=1) * gain_ref[:, ps]).astype(BF16)
        o_ref[rows, ps] = o * _silu(og_ref[rows, ps])


PROJ_F, PROJ_Q, PROJ_I, PROJ_OG, PROJ_POOL = 0, 1, 2, 3, 4


def _proj_hgrn_kernel(x_ref, g_ref, w_ref, lb_ref, gain_ref, mall_ref, lev_ref, *rest,
                      layer, nlev, heads, n_cast, tiles_per_seq, chunks):
    cast_src = rest[:n_cast]
    zr_ref, bo_ref = rest[n_cast:n_cast + 2]
    cast_dst = rest[n_cast + 2:2 * n_cast + 2]
    u_ref, zs_ref, fs_ref, state_ref, lbs_ref = rest[2 * n_cast + 2:]
    i, j = pl.program_id(0), pl.program_id(1)

    def project():
        return _dot(u_ref[...], w_ref[...])

    def convert_weights():
        for src, dst in zip(cast_src, cast_dst):
            dst[...] = src[...].astype(BF16)

    @pl.when(j == PROJ_F)
    def _():
        @pl.when(i == 0)
        def _():
            _lower_bound_tables(lb_ref, lbs_ref, layer)

        @pl.when(i % tiles_per_seq == 0)
        def _():
            state_ref[...] = jnp.zeros_like(state_ref)

        u_ref[...] = _rmsnorm(x_ref[...], g_ref[...]).astype(BF16)
        fs_ref[...] = project()
        convert_weights()

    @pl.when((j > PROJ_F) & (j < PROJ_POOL))
    def _():
        zs_ref[j - PROJ_Q] = project().astype(BF16)
        convert_weights()

    @pl.when((j >= PROJ_POOL) & (j < PROJ_POOL + chunks))
    def _():
        def project_slice(k, n):
            wc = w_ref.shape[1] // n
            cols = slice(k * wc, (k + 1) * wc)
            zr_ref[:, cols] = _dot(u_ref[...], w_ref[:, cols]).astype(BF16)
            if k == n - 1:
                convert_weights()

        rows = pl.ds(pl.multiple_of((j - PROJ_POOL) * HGRN_CHUNK, HGRN_CHUNK), HGRN_CHUNK)
        _hgrn_chunk(zs_ref, fs_ref, rows, gain_ref, mall_ref, lev_ref, lbs_ref, state_ref, bo_ref, nlev, heads,
                    project_slice)

    @pl.when(j >= PROJ_POOL + chunks)
    def _():
        zr_ref[...] = project().astype(BF16)
        convert_weights()


def _cast_plan(w, layer, steps, step):
    _, r, cols = w.shape
    csplit = 1
    while r % ((steps // csplit) * BF16_ROW_TILE) != 0:
        csplit *= 2
    assert steps % csplit == 0 and cols % (csplit * LANES) == 0, (w.shape, steps)
    br, bc = r // (steps // csplit), cols // csplit
    in_spec = pl.BlockSpec((None, br, bc), lambda i, j: (layer, step(i, j) // csplit, step(i, j) % csplit))
    out_spec = pl.BlockSpec((br, bc), lambda i, j: (step(i, j) // csplit, step(i, j) % csplit))
    return in_spec, out_spec, jax.ShapeDtypeStruct((r, cols), BF16)


def _proj_hgrn(x, g_all, w, lb_raw, gain_all, layer, seq, mall, lev, nlev, tm, col_blocks, casts):
    m, d = x.shape
    hd = lb_raw.shape[1]
    depth = lb_raw.shape[0]
    heads = hd // HGRN_HEAD
    nb = len(col_blocks)
    chunks = tm // HGRN_CHUNK
    n_out = nb - PROJ_POOL
    assert PROJ_POOL + chunks <= nb

    def select(j, table):
        out = jnp.int32(table[-1])
        for k in range(len(table) - 2, -1, -1):
            out = jnp.where(j == k, table[k], out)
        return out

    out_blocks = [n_out - 1] * (PROJ_POOL + 1) + list(range(n_out - 1))
    cast_steps = nb - 1
    step = lambda i, j: i * cast_steps + jnp.minimum(j, cast_steps - 1)
    plans = [_cast_plan(cw, cl, (m // tm) * cast_steps, step) for cw, cl in casts]
    return pl.pallas_call(
        functools.partial(_proj_hgrn_kernel, layer=layer, nlev=nlev, heads=heads, n_cast=len(casts),
                          tiles_per_seq=seq // tm, chunks=chunks),
        grid=(m // tm, nb),
        in_specs=[
            pl.BlockSpec((tm, d), lambda i, j: (i, 0)),
            pl.BlockSpec((None, 1, d), lambda i, j: (layer, 0, 0)),
            pl.BlockSpec((d, hd), lambda i, j: (0, select(j, col_blocks))),
            _resident((depth, hd), lambda i, j: (0, 0)),
            pl.BlockSpec((None, 1, hd), lambda i, j: (layer, 0, 0)),
            _resident(mall.shape, lambda i, j: (0, 0)),
            _resident(lev.shape, lambda i, j: (0, 0)),
        ] + [p[0] for p in plans],
        out_specs=[
            pl.BlockSpec((tm, hd), lambda i, j: (i, select(j, out_blocks))),
            pl.BlockSpec((tm, hd), lambda i, j: (i, 0)),
        ] + [p[1] for p in plans],
        out_shape=[jax.ShapeDtypeStruct((m, n_out * hd), BF16), jax.ShapeDtypeStruct((m, hd), BF16)]
        + [p[2] for p in plans],
        scratch_shapes=[pltpu.VMEM((tm, d), BF16),
                        pltpu.VMEM((3, tm, hd), BF16),
                        pltpu.VMEM((tm, hd), F32),
                        pltpu.VMEM((heads, HGRN_HEAD, HGRN_HEAD), F32),
                        pltpu.VMEM((8, hd), F32)],
        compiler_params=_params(("arbitrary", "arbitrary")),
        name="proj_hgrn",
    )(x, g_all, w, lb_raw, gain_all, mall, lev, *[cw for cw, _ in casts])


def _mix_kernel(zp_ref, halo_ref, ga_ref, gb_ref, bo_ref, h_ref, wg_ref, sc_ref,
                wbp_ref, wbh_ref, wmo_ref, o_ref, *, tm):
    rows = tm // MIX_SLABS
    gd = zp_ref.shape[1] // len(POOL_WINDOWS)
    for part in range(MIX_SLABS):
        rs = slice(part * rows, (part + 1) * rows)
        t0 = pl.program_id(1) * tm + part * rows
        x = zp_ref[rs, :].astype(F32)
        if part == 0:
            halo = jnp.where(t0 > 0, halo_ref[...].astype(F32), 0.0)
        else:
            halo = zp_ref[part * rows - POOL_HALO:part * rows, :].astype(F32)
        ext = jnp.concatenate([halo, x], axis=0)
        pos = lax.broadcasted_iota(jnp.int32, (rows, 1), 0) + t0
        outs = []
        for g, wdw in enumerate(POOL_WINDOWS):
            cols = slice(g * gd, (g + 1) * gd)
            e = ext[:, cols]
            sh = 1
            while sh < wdw:
                e = e + pltpu.roll(e, sh, axis=0)
                sh *= 2
            cnt = jnp.minimum(pos + 1, wdw).astype(F32)
            pooled = (e[POOL_HALO:] / cnt - x[:, cols]).astype(BF16)
            outs.append((_dot(pooled, wg_ref[g]) * sc_ref[:, cols]).astype(BF16))
        a_out = jnp.concatenate(outs, axis=1)
        merged = _sigmoid(ga_ref[rs, :].astype(F32)) * _dot(a_out, wbp_ref[...])
        merged = merged + _sigmoid(gb_ref[rs, :].astype(F32)) * _dot(bo_ref[rs, :], wbh_ref[...])
        o_ref[rs, :] = h_ref[rs, :] + _dot(merged.astype(BF16), wmo_ref[...])


def _mix(zr, b_out, h, wg_all, sc_all, wbp, wbh, wmo, layer, batch, seq, tm):
    m, d = h.shape
    pd = b_out.shape[1]
    pool_blk = 2 * d // pd
    ng, gd = wg_all.shape[1], wg_all.shape[2]
    nt = seq // tm
    row = lambda b, t: b * nt + t
    hb = tm // POOL_HALO
    return pl.pallas_call(
        functools.partial(_mix_kernel, tm=tm),
        grid=(batch, nt),
        in_specs=[
            pl.BlockSpec((tm, pd), lambda b, t: (row(b, t), pool_blk)),
            pl.BlockSpec((POOL_HALO, pd), lambda b, t: (jnp.maximum(row(b, t) * hb - 1, 0), pool_blk)),
            pl.BlockSpec((tm, d), lambda b, t: (row(b, t), 0)),
            pl.BlockSpec((tm, d), lambda b, t: (row(b, t), 1)),
            pl.BlockSpec((tm, pd), lambda b, t: (row(b, t), 0)),
            pl.BlockSpec((tm, d), lambda b, t: (row(b, t), 0)),
            _resident((None, ng, gd, gd), lambda b, t: (layer, 0, 0, 0)),
            _resident((None, 1, pd), lambda b, t: (layer, 0, 0)),
            _resident((pd, d), lambda b, t: (0, 0)),
            _resident((pd, d), lambda b, t: (0, 0)),
            _resident((d, d), lambda b, t: (0, 0)),
        ],
        out_specs=pl.BlockSpec((tm, d), lambda b, t: (row(b, t), 0)),
        out_shape=jax.ShapeDtypeStruct((m, d), F32),
        compiler_params=_params(("parallel", "parallel")),
        name="mix",
    )(zr, zr, zr, zr, b_out, h, wg_all, sc_all, wbp, wbh, wmo)


def _xattn_kernel(h_ref, g_ref, wq_ref, kv_ref, wo_ref, o_ref, att_ref):
    h = h_ref[...]
    d = h.shape[1]
    dh = d // X_HEADS
    u = _rmsnorm(h, g_ref[...]).astype(BF16)
    q = _dot(u, wq_ref[...]).astype(BF16)
    for hh in range(X_HEADS):
        cols = slice(hh * dh, (hh + 1) * dh)
        s = _dot_nt(q[:, cols], kv_ref[:, cols]) * (dh ** -0.5)
        p = jnp.exp(s - jnp.max(s, axis=-1, keepdims=True))
        p = p / jnp.sum(p, axis=-1, keepdims=True)
        att_ref[:, cols] = _dot(p.astype(BF16), kv_ref[:, d + hh * dh:d + (hh + 1) * dh]).astype(BF16)
    o_ref[...] = h + _dot(att_ref[...], wo_ref[...])


def _xattn(h, gain_all, wq, kv, wo, layer, batch, seq, mem_len, tm):
    m, d = h.shape
    nt = seq // tm
    row = lambda b, t: b * nt + t
    return pl.pallas_call(
        _xattn_kernel,
        grid=(batch, nt),
        in_specs=[
            pl.BlockSpec((tm, d), lambda b, t: (row(b, t), 0)),
            pl.BlockSpec((None, 1, d), lambda b, t: (layer, 0, 0)),
            _resident((d, d), lambda b, t: (0, 0)),
            pl.BlockSpec((mem_len, 2 * d), lambda b, t: (b, 0)),
            _resident((d, d), lambda b, t: (0, 0)),
        ],
        out_specs=pl.BlockSpec((tm, d), lambda b, t: (row(b, t), 0)),
        out_shape=jax.ShapeDtypeStruct((m, d), F32),
        scratch_shapes=[pltpu.VMEM((tm, d), BF16)],
        compiler_params=_params(("parallel", "parallel")),
        name="xattn",
    )(h, gain_all, wq, kv, wo)


def _ffn_kernel(h_ref, g_ref, wg_ref, wu_ref, wo_ref, *rest, final):
    if final:
        gf_ref, o_ref, u_ref = rest
    else:
        o_ref, u_ref = rest
    f = pl.program_id(1)

    @pl.when(f == 0)
    def _():
        h = h_ref[...]
        u_ref[...] = _rmsnorm(h, g_ref[...]).astype(BF16)
        o_ref[...] = h

    u = u_ref[...]
    gate = _dot(u, wg_ref[...])
    up = _dot(u, wu_ref[...])
    act = (gate * _sigmoid(gate) * up).astype(BF16)
    o_ref[...] += _dot(act, wo_ref[...])

    if final:
        @pl.when(f == pl.num_programs(1) - 1)
        def _():
            o_ref[...] = _rmsnorm(o_ref[...], gf_ref[...])


def _ffn(h, gain_all, w_in, w_out, layer, tm, tf, final_gain=None):
    m, d = h.shape
    dff = w_out.shape[0]
    nf = dff // tf
    final = final_gain is not None
    in_specs = [
        pl.BlockSpec((tm, d), lambda i, f: (i, 0)),
        pl.BlockSpec((None, 1, d), lambda i, f: (layer, 0, 0)),
        pl.BlockSpec((d, tf), lambda i, f: (0, f)),
        pl.BlockSpec((d, tf), lambda i, f: (0, nf + f)),
        pl.BlockSpec((tf, d), lambda i, f: (f, 0)),
    ]
    args = [h, gain_all, w_in, w_in, w_out]
    if final:
        in_specs.append(pl.BlockSpec((1, d), lambda i, f: (0, 0)))
        args.append(final_gain)
    return pl.pallas_call(
        functools.partial(_ffn_kernel, final=final),
        grid=(m // tm, nf),
        in_specs=in_specs,
        out_specs=pl.BlockSpec((tm, d), lambda i, f: (i, 0)),
        out_shape=jax.ShapeDtypeStruct((m, d), F32),
        scratch_shapes=[pltpu.VMEM((tm, d), BF16)],
        compiler_params=_params(("parallel", "arbitrary")),
        name="ffn",
    )(*args)


def _tiles(tokens, seq, mem_tokens, hd):
    return {
        "proj_rows": min(512, seq),
        "mix_rows": min(512, seq),
        "kv_rows": min(1024, mem_tokens),
        "kv_cols": hd,
        "xattn_rows": min(512, seq),
        "ffn_rows": min(1024, tokens),
        "ffn_cols": 512,
    }


def kernel(x, mem, w_in, w_pool_group, pool_scale, hgrn_lower_bounds, hgrn_norm, w_branch_pool,
           w_branch_hgrn, w_mix_out, norm_mix, norm_mem, norm_cross, w_xq, w_xkv, w_xo, norm_ffn,
           w_ffn_in, w_ffn_out, norm_final):
    batch, seq, d = x.shape
    mem_len = mem.shape[1]
    depth = w_in.shape[0]
    pd = w_branch_pool.shape[1]
    hd = hgrn_lower_bounds.shape[1]

    f_lo = pd + hd
    win = w_in[0].astype(BF16)
    wg = w_pool_group.astype(BF16)
    layer_weights = (w_branch_pool, w_branch_hgrn, w_mix_out, w_xq, w_xkv, w_xo, w_ffn_in, w_ffn_out)

    as_rows = lambda a: a.reshape(a.shape[0], 1, a.shape[1])
    g_mix, g_mem, g_cross, g_ffn = map(as_rows, (norm_mix, norm_mem, norm_cross, norm_ffn))
    g_hgrn, sc_pool = as_rows(hgrn_norm), as_rows(pool_scale)
    g_final = norm_final.reshape(1, d)

    mall_np, lev_np, nlev = _hgrn_tables(HGRN_CHUNK)
    mall = jnp.asarray(mall_np, BF16)
    lev = jnp.asarray(lev_np)

    t = _tiles(batch * seq, seq, batch * mem_len, hd)
    h = x.reshape(batch * seq, d)
    mem2 = mem.reshape(batch * mem_len, d)
    blk = lambda col: col // hd
    gates = list(range(blk(pd + 4 * hd), blk(w_in.shape[2])))
    col_blocks = [blk(f_lo), blk(pd), blk(pd + 2 * hd), blk(pd + 3 * hd), 0] + gates
    for l in range(depth):
        casts = [(w, l) for w in layer_weights] + ([(w_in, l + 1)] if l + 1 < depth else [])
        zr, b_out, wbp, wbh, wmo, wxq, wxkv, wxo, wfi, wfo, *nxt = _proj_hgrn(
            h, g_mix, win, hgrn_lower_bounds, g_hgrn, l, seq, mall, lev, nlev, t["proj_rows"], col_blocks, casts)
        win = nxt[0] if nxt else None
        h = _mix(zr, b_out, h, wg, sc_pool, wbp, wbh, wmo, l, batch, seq, t["mix_rows"])
        kv = _norm_mm(mem2, g_mem, wxkv, l, BF16, t["kv_rows"], t["kv_cols"])
        h = _xattn(h, g_cross, wxq, kv, wxo, l, batch, seq, mem_len, t["xattn_rows"])
        h = _ffn(h, g_ffn, wfi, wfo, l, t["ffn_rows"], t["ffn_cols"], g_final if l == depth - 1 else None)
    return h.reshape(batch, seq, d)
```

```python
import functools

import numpy as np
import jax
import jax.numpy as jnp
from jax import lax
from jax.experimental import pallas as pl
from jax.experimental.pallas import tpu as pltpu

F32 = jnp.float32
BF16 = jnp.bfloat16
EPS = 1e-6
LOG2E = 1.4426950408889634

SUBLANES = 8
LANES = 128
BF16_ROW_TILE = 16

POOL_WINDOWS = (2, 4, 8, 16)
POOL_HALO = 16
MIX_SLABS = 2
HGRN_HEAD = 128
HGRN_CHUNK = 128
X_HEADS = 4

VMEM_LIMIT = 56 * 1024 * 1024


def _params(sem):
    return pltpu.CompilerParams(dimension_semantics=sem, vmem_limit_bytes=VMEM_LIMIT)


def _resident(block_shape, index_map):
    return pl.BlockSpec(block_shape, index_map, pipeline_mode=pl.Buffered(1))


def _rmsnorm(x, gain):
    ms = jnp.mean(x * x, axis=-1, keepdims=True)
    return x * lax.rsqrt(ms + EPS) * gain


def _dot(a, b):
    return jnp.dot(a, b, preferred_element_type=F32)


def _dot_nt(a, b):
    return lax.dot_general(a, b, (((1,), (1,)), ((), ())), preferred_element_type=F32)


def _dot_tn(a, b):
    return lax.dot_general(a, b, (((0,), (0,)), ((), ())), preferred_element_type=F32)


def _sigmoid(x):
    return 1.0 / (1.0 + jnp.exp(-x))


def _norm_mm_kernel(x_ref, g_ref, w_ref, o_ref, u_ref):
    @pl.when(pl.program_id(1) == 0)
    def _():
        u_ref[...] = _rmsnorm(x_ref[...], g_ref[...]).astype(BF16)

    o_ref[...] = _dot(u_ref[...], w_ref[...]).astype(o_ref.dtype)


def _norm_mm(x, gain_all, w, layer, out_dtype, tm, tn):
    m, d = x.shape
    n = w.shape[-1]
    return pl.pallas_call(
        _norm_mm_kernel,
        grid=(m // tm, n // tn),
        in_specs=[
            pl.BlockSpec((tm, d), lambda i, j: (i, 0)),
            pl.BlockSpec((None, 1, d), lambda i, j: (layer, 0, 0)),
            pl.BlockSpec((d, tn), lambda i, j: (0, j)),
        ],
        out_specs=pl.BlockSpec((tm, tn), lambda i, j: (i, j)),
        out_shape=jax.ShapeDtypeStruct((m, n), out_dtype),
        scratch_shapes=[pltpu.VMEM((tm, d), BF16)],
        compiler_params=_params(("parallel", "arbitrary")),
        name="norm_mm",
    )(x, gain_all, w)


def _in_proj_kernel(x_ref, g_ref, w_ref, zr_ref, zf_ref, u_ref):
    @pl.when(pl.program_id(1) == 0)
    def _():
        u_ref[...] = _rmsnorm(x_ref[...], g_ref[...]).astype(BF16)

    z = _dot(u_ref[...], w_ref[...])
    zr_ref[...] = z.astype(BF16)
    zf_ref[...] = z


def _in_proj(x, gain_all, w, layer, tm, tn, f_lo):
    m, d = x.shape
    n = w.shape[-1]
    nb = n // tn
    f_blk = f_lo // tn

    def w_col(j):
        return jnp.where(j == nb - 1, f_blk, j + (j >= f_blk).astype(jnp.int32))

    return pl.pallas_call(
        _in_proj_kernel,
        grid=(m // tm, nb),
        in_specs=[
            pl.BlockSpec((tm, d), lambda i, j: (i, 0)),
            pl.BlockSpec((None, 1, d), lambda i, j: (layer, 0, 0)),
            pl.BlockSpec((d, tn), lambda i, j: (0, w_col(j))),
        ],
        out_specs=[
            pl.BlockSpec((tm, tn), lambda i, j: (i, j)),
            pl.BlockSpec((tm, tn), lambda i, j: (i, 0)),
        ],
        out_shape=[jax.ShapeDtypeStruct((m, n), BF16), jax.ShapeDtypeStruct((m, tn), F32)],
        scratch_shapes=[pltpu.VMEM((tm, d), BF16)],
        compiler_params=_params(("parallel", "arbitrary")),
        name="in_proj",
    )(x, gain_all, w)


def _hgrn_tables(c):
    nlev = int(np.log2(c))
    t = np.arange(c)[:, None]
    j = np.arange(c)[None, :]
    blocks = [(j <= t)]
    for l in range(1, nlev + 1):
        m = 2 ** (l - 1)
        r = (t // (2 * m)) * (2 * m) + m - 1
        right = (t % (2 * m)) >= m
        blocks.append(np.where(right, (j > r) & (j <= t), (j > t) & (j <= r)))
    blocks.append(j > t)
    mall = np.concatenate(blocks, axis=0).astype(np.float32)
    mall = np.concatenate([mall, mall], axis=1)
    x = t ^ j
    lev = np.zeros((c, c), np.int32)
    low = j < t
    lev[low] = np.floor(np.log2(x[low])).astype(np.int32) + 1
    lev[np.arange(c), np.arange(c)] = nlev + 1
    return mall, lev, nlev


def _silu(z):
    return z / (1.0 + jnp.exp(-z))


def _hgrn_kernel(lb_ref, q_ref, f_ref, i_ref, og_ref, gain_ref, mall_ref, lev_ref, *rest,
                 layer, nlev, heads, n_cast):
    cast_src = rest[:n_cast]
    o_ref = rest[n_cast]
    cast_dst = rest[n_cast + 1:2 * n_cast + 1]
    state_ref, lbs_ref = rest[2 * n_cast + 1:]
    c = HGRN_CHUNK
    hw = HGRN_HEAD
    pw = 2 * hw

    @pl.when(pl.program_id(1) == 0)
    def _():
        state_ref[...] = jnp.zeros_like(state_ref)
        x = lb_ref[...]
        e = jnp.exp(x - jnp.max(x, axis=0, keepdims=True))
        sm = e / jnp.sum(e, axis=0, keepdims=True)
        lb = jnp.zeros_like(sm[0:1])
        for r in range(1, layer + 1):
            lb = lb + sm[r:r + 1]
        lbs_ref[0:1, :] = jnp.log2(lb)
        lbs_ref[1:2, :] = jnp.log2(1.0 - lb)
        lbs_ref[2:3, :] = 1.0 - lb

    sub = SUBLANES
    nblk = c // sub
    lev = lev_ref[...]
    lev_blk = [lev[i * sub:(i + 1) * sub] for i in range(nblk)]
    row_in_blk = lax.broadcasted_iota(jnp.int32, (sub, hw), 0)

    def is_right(row, m):
        return (row & m) != 0

    def level_operand(q_rows, k_rows, w_l, m):
        if m >= sub:
            qk = [q_rows[i] if is_right(i * sub, m) else k_rows[i] for i in range(nblk)]
        else:
            pick_q = is_right(row_in_blk, m)
            qk = [jnp.where(pick_q, q_rows[i], k_rows[i]) for i in range(nblk)]
        return (jnp.concatenate(qk, axis=0) * w_l).astype(BF16)

    def one_head(q_h, k_h, w_h, v, hidx):
        q_rows = [q_h[i * sub:(i + 1) * sub] for i in range(nblk)]
        k_rows = [k_h[i * sub:(i + 1) * sub] for i in range(nblk)]
        s0 = _dot_nt(q_h.astype(BF16), k_h.astype(BF16))
        s_lv = []
        for l in range(1, nlev + 1):
            x = level_operand(q_rows, k_rows, w_h[l * c:(l + 1) * c], 2 ** (l - 1))
            s_lv.append(_dot_nt(x, x))
        s_rows = [jnp.where(lev_blk[i] == nlev + 1, s0[i * sub:(i + 1) * sub], 0.0) for i in range(nblk)]
        for l in range(1, nlev + 1):
            m = 2 ** (l - 1)
            s_l = s_lv[l - 1]
            for i in range(nblk):
                if m < sub or is_right(i * sub, m):
                    s_rows[i] = jnp.where(lev_blk[i] == l, s_l[i * sub:(i + 1) * sub], s_rows[i])
        s = jnp.concatenate(s_rows, axis=0).astype(BF16)
        state_t = state_ref[hidx]
        o = _dot(s, v) + _dot_nt((q_h * w_h[0:c]).astype(BF16), state_t.astype(BF16))
        k_dec = (k_h * w_h[(nlev + 1) * c:(nlev + 2) * c]).astype(BF16)
        state_ref[hidx] = state_t * w_h[c - 1:c] + _dot_tn(v, k_dec)
        return o * lax.rsqrt(jnp.mean(o * o, axis=1, keepdims=True) + EPS)

    def gates(hp):
        ps = slice(hp * pw, (hp + 1) * pw)
        q = _silu(q_ref[:, ps]).astype(F32)
        log2_lb = lbs_ref[0:1, ps]
        log2_1m_lb = lbs_ref[1:2, ps]
        one_m_lb = lbs_ref[2:3, ps]
        z2 = f_ref[:, ps] * LOG2E
        e = jnp.exp2(-jnp.abs(z2))
        ope = 1.0 + e
        rcp = 1.0 / ope
        k = one_m_lb * jnp.where(z2 >= 0, e * rcp, rcp)
        cc = log2_1m_lb + (jnp.minimum(z2, 0.0) - jnp.log2(ope))
        g = jnp.maximum(log2_lb, cc) + jnp.log2(1.0 + jnp.exp2(-jnp.abs(log2_lb - cc)))
        g_hi = g.astype(BF16)
        g_lo = (g - g_hi.astype(F32)).astype(BF16)
        w = jnp.exp2(_dot(mall_ref[...], jnp.concatenate([g_hi, g_lo], axis=0)))
        return q, k, w

    nxt = gates(0)
    for hp in range(heads // 2):
        ps = slice(hp * pw, (hp + 1) * pw)
        q, k, w = nxt
        outs = []
        for j in range(2):
            cols = slice(j * hw, (j + 1) * hw)
            v = i_ref[:, hp * pw + j * hw:hp * pw + (j + 1) * hw]
            outs.append(one_head(q[:, cols], k[:, cols], w[:, cols], v, 2 * hp + j))
            if j == 0 and hp + 1 < heads // 2:
                nxt = gates(hp + 1)
        o = (jnp.concatenate(outs, axis=1) * gain_ref[:, ps]).astype(BF16)
        o_ref[:, ps] = o * _silu(og_ref[:, ps])
        for src, dst in list(zip(cast_src, cast_dst))[hp::heads // 2]:
            dst[...] = src[...].astype(BF16)


def _cast_plan(w, layer, steps):
    _, r, cols = w.shape
    csplit = 1
    while r % ((steps // csplit) * BF16_ROW_TILE) != 0:
        csplit *= 2
    assert steps % csplit == 0 and cols % (csplit * LANES) == 0, (w.shape, steps)
    br, bc = r // (steps // csplit), cols // csplit
    in_spec = lambda step: pl.BlockSpec((None, br, bc), lambda b, t: (layer, step(b, t) // csplit, step(b, t) % csplit))
    out_spec = lambda step: pl.BlockSpec((br, bc), lambda b, t: (step(b, t) // csplit, step(b, t) % csplit))
    return in_spec, out_spec, jax.ShapeDtypeStruct((r, cols), BF16)


def _hgrn(zr, zf, lb_raw, gain_all, layer, batch, seq, mall, lev, nlev, casts):
    m = zr.shape[0]
    hd = zf.shape[1]
    depth = lb_raw.shape[0]
    c = HGRN_CHUNK
    nc = seq // c
    heads = hd // HGRN_HEAD
    row = lambda b, t: b * nc + t
    plans = [_cast_plan(w, wl, batch * nc) for w, wl in casts]
    return pl.pallas_call(
        functools.partial(_hgrn_kernel, layer=layer, nlev=nlev, heads=heads, n_cast=len(casts)),
        grid=(batch, nc),
        in_specs=[
            _resident((depth, hd), lambda b, t: (0, 0)),
            pl.BlockSpec((c, hd), lambda b, t: (row(b, t), 1)),
            pl.BlockSpec((c, hd), lambda b, t: (row(b, t), 0)),
            pl.BlockSpec((c, hd), lambda b, t: (row(b, t), 2)),
            pl.BlockSpec((c, hd), lambda b, t: (row(b, t), 3)),
            pl.BlockSpec((None, 1, hd), lambda b, t: (layer, 0, 0)),
            _resident(mall.shape, lambda b, t: (0, 0)),
            _resident(lev.shape, lambda b, t: (0, 0)),
        ] + [p[0](row) for p in plans],
        out_specs=[pl.BlockSpec((c, hd), lambda b, t: (row(b, t), 0))] + [p[1](row) for p in plans],
        out_shape=[jax.ShapeDtypeStruct((m, hd), BF16)] + [p[2] for p in plans],
        scratch_shapes=[pltpu.VMEM((heads, HGRN_HEAD, HGRN_HEAD), F32),
                        pltpu.VMEM((8, hd), F32)],
        compiler_params=_params(("parallel", "arbitrary")),
        name="hgrn",
    )(lb_raw, zr, zf, zr, zr, gain_all, mall, lev, *[w for w, _ in casts])


def _mix_kernel(zp_ref, halo_ref, ga_ref, gb_ref, bo_ref, h_ref, wg_ref, sc_ref,
                wbp_ref, wbh_ref, wmo_ref, o_ref, *, tm):
    rows = tm // MIX_SLABS
    gd = zp_ref.shape[1] // len(POOL_WINDOWS)
    for part in range(MIX_SLABS):
        rs = slice(part * rows, (part + 1) * rows)
        t0 = pl.program_id(1) * tm + part * rows
        x = zp_ref[rs, :].astype(F32)
        if part == 0:
            halo = jnp.where(t0 > 0, halo_ref[...].astype(F32), 0.0)
        else:
            halo = zp_ref[part * rows - POOL_HALO:part * rows, :].astype(F32)
        ext = jnp.concatenate([halo, x], axis=0)
        pos = lax.broadcasted_iota(jnp.int32, (rows, 1), 0) + t0
        outs = []
        for g, wdw in enumerate(POOL_WINDOWS):
            cols = slice(g * gd, (g + 1) * gd)
            e = ext[:, cols]
            sh = 1
            while sh < wdw:
                e = e + pltpu.roll(e, sh, axis=0)
                sh *= 2
            cnt = jnp.minimum(pos + 1, wdw).astype(F32)
            pooled = (e[POOL_HALO:] / cnt - x[:, cols]).astype(BF16)
            outs.append((_dot(pooled, wg_ref[g]) * sc_ref[:, cols]).astype(BF16))
        a_out = jnp.concatenate(outs, axis=1)
        merged = _sigmoid(ga_ref[rs, :].astype(F32)) * _dot(a_out, wbp_ref[...])
        merged = merged + _sigmoid(gb_ref[rs, :].astype(F32)) * _dot(bo_ref[rs, :], wbh_ref[...])
        o_ref[rs, :] = h_ref[rs, :] + _dot(merged.astype(BF16), wmo_ref[...])


def _mix(zr, b_out, h, wg_all, sc_all, wbp, wbh, wmo, layer, batch, seq, tm):
    m, d = h.shape
    pd = b_out.shape[1]
    ng, gd = wg_all.shape[1], wg_all.shape[2]
    nt = seq // tm
    row = lambda b, t: b * nt + t
    hb = tm // POOL_HALO
    return pl.pallas_call(
        functools.partial(_mix_kernel, tm=tm),
        grid=(batch, nt),
        in_specs=[
            pl.BlockSpec((tm, pd), lambda b, t: (row(b, t), 0)),
            pl.BlockSpec((POOL_HALO, pd), lambda b, t: (jnp.maximum(row(b, t) * hb - 1, 0), 0)),
            pl.BlockSpec((tm, d), lambda b, t: (row(b, t), 2)),
            pl.BlockSpec((tm, d), lambda b, t: (row(b, t), 3)),
            pl.BlockSpec((tm, pd), lambda b, t: (row(b, t), 0)),
            pl.BlockSpec((tm, d), lambda b, t: (row(b, t), 0)),
            _resident((None, ng, gd, gd), lambda b, t: (layer, 0, 0, 0)),
            _resident((None, 1, pd), lambda b, t: (layer, 0, 0)),
            _resident((pd, d), lambda b, t: (0, 0)),
            _resident((pd, d), lambda b, t: (0, 0)),
            _resident((d, d), lambda b, t: (0, 0)),
        ],
        out_specs=pl.BlockSpec((tm, d), lambda b, t: (row(b, t), 0)),
        out_shape=jax.ShapeDtypeStruct((m, d), F32),
        compiler_params=_params(("parallel", "parallel")),
        name="mix",
    )(zr, zr, zr, zr, b_out, h, wg_all, sc_all, wbp, wbh, wmo)


def _xattn_kernel(h_ref, g_ref, wq_ref, kv_ref, wo_ref, o_ref, att_ref):
    h = h_ref[...]
    d = h.shape[1]
    dh = d // X_HEADS
    u = _rmsnorm(h, g_ref[...]).astype(BF16)
    q = _dot(u, wq_ref[...]).astype(BF16)
    for hh in range(X_HEADS):
        cols = slice(hh * dh, (hh + 1) * dh)
        s = _dot_nt(q[:, cols], kv_ref[:, cols]) * (dh ** -0.5)
        p = jnp.exp(s - jnp.max(s, axis=-1, keepdims=True))
        p = p / jnp.sum(p, axis=-1, keepdims=True)
        att_ref[:, cols] = _dot(p.astype(BF16), kv_ref[:, d + hh * dh:d + (hh + 1) * dh]).astype(BF16)
    o_ref[...] = h + _dot(att_ref[...], wo_ref[...])


def _xattn(h, gain_all, wq, kv, wo, layer, batch, seq, mem_len, tm):
    m, d = h.shape
    nt = seq // tm
    row = lambda b, t: b * nt + t
    return pl.pallas_call(
        _xattn_kernel,
        grid=(batch, nt),
        in_specs=[
            pl.BlockSpec((tm, d), lambda b, t: (row(b, t), 0)),
            pl.BlockSpec((None, 1, d), lambda b, t: (layer, 0, 0)),
            _resident((d, d), lambda b, t: (0, 0)),
            pl.BlockSpec((mem_len, 2 * d), lambda b, t: (b, 0)),
            _resident((d, d), lambda b, t: (0, 0)),
        ],
        out_specs=pl.BlockSpec((tm, d), lambda b, t: (row(b, t), 0)),
        out_shape=jax.ShapeDtypeStruct((m, d), F32),
        scratch_shapes=[pltpu.VMEM((tm, d), BF16)],
        compiler_params=_params(("parallel", "parallel")),
        name="xattn",
    )(h, gain_all, wq, kv, wo)


def _ffn_kernel(h_ref, g_ref, wg_ref, wu_ref, wo_ref, *rest, final):
    if final:
        gf_ref, o_ref, u_ref = rest
    else:
        o_ref, u_ref = rest
    f = pl.program_id(1)

    def partial_sum():
        u = u_ref[...]
        gate = _dot(u, wg_ref[...])
        up = _dot(u, wu_ref[...])
        act = (gate * _sigmoid(gate) * up).astype(BF16)
        return _dot(act, wo_ref[...])

    @pl.when(f == 0)
    def _():
        h = h_ref[...]
        u_ref[...] = _rmsnorm(h, g_ref[...]).astype(BF16)
        o_ref[...] = h + partial_sum()

    @pl.when(f > 0)
    def _():
        o_ref[...] += partial_sum()

    if final:
        @pl.when(f == pl.num_programs(1) - 1)
        def _():
            o_ref[...] = _rmsnorm(o_ref[...], gf_ref[...])


def _ffn(h, gain_all, w_in, w_out, layer, tm, tf, final_gain=None):
    m, d = h.shape
    dff = w_out.shape[0]
    nf = dff // tf
    final = final_gain is not None
    in_specs = [
        pl.BlockSpec((tm, d), lambda i, f: (i, 0)),
        pl.BlockSpec((None, 1, d), lambda i, f: (layer, 0, 0)),
        pl.BlockSpec((d, tf), lambda i, f: (0, f)),
        pl.BlockSpec((d, tf), lambda i, f: (0, nf + f)),
        pl.BlockSpec((tf, d), lambda i, f: (f, 0)),
    ]
    args = [h, gain_all, w_in, w_in, w_out]
    if final:
        in_specs.append(pl.BlockSpec((1, d), lambda i, f: (0, 0)))
        args.append(final_gain)
    return pl.pallas_call(
        functools.partial(_ffn_kernel, final=final),
        grid=(m // tm, nf),
        in_specs=in_specs,
        out_specs=pl.BlockSpec((tm, d), lambda i, f: (i, 0)),
        out_shape=jax.ShapeDtypeStruct((m, d), F32),
        scratch_shapes=[pltpu.VMEM((tm, d), BF16)],
        compiler_params=_params(("parallel", "arbitrary")),
        name="ffn",
    )(*args)


def _tiles(tokens, seq, mem_tokens, hd):
    return {
        "proj_rows": min(1024, tokens),
        "mix_rows": min(512, seq),
        "kv_rows": min(1024, mem_tokens),
        "kv_cols": hd,
        "xattn_rows": min(512, seq),
        "ffn_rows": min(1024, tokens),
        "ffn_cols": 512,
    }


def kernel(x, mem, w_in, w_pool_group, pool_scale, hgrn_lower_bounds, hgrn_norm, w_branch_pool,
           w_branch_hgrn, w_mix_out, norm_mix, norm_mem, norm_cross, w_xq, w_xkv, w_xo, norm_ffn,
           w_ffn_in, w_ffn_out, norm_final):
    batch, seq, d = x.shape
    mem_len = mem.shape[1]
    depth = w_in.shape[0]
    pd = w_branch_pool.shape[1]
    hd = hgrn_lower_bounds.shape[1]

    f_lo = pd + hd
    win = w_in[0].astype(BF16)
    wg = w_pool_group.astype(BF16)
    layer_weights = (w_branch_pool, w_branch_hgrn, w_mix_out, w_xq, w_xkv, w_xo, w_ffn_in, w_ffn_out)

    as_rows = lambda a: a.reshape(a.shape[0], 1, a.shape[1])
    g_mix, g_mem, g_cross, g_ffn = map(as_rows, (norm_mix, norm_mem, norm_cross, norm_ffn))
    g_hgrn, sc_pool = as_rows(hgrn_norm), as_rows(pool_scale)
    g_final = norm_final.reshape(1, d)

    mall_np, lev_np, nlev = _hgrn_tables(HGRN_CHUNK)
    mall = jnp.asarray(mall_np, BF16)
    lev = jnp.asarray(lev_np)

    t = _tiles(batch * seq, seq, batch * mem_len, hd)
    h = x.reshape(batch * seq, d)
    mem2 = mem.reshape(batch * mem_len, d)
    for l in range(depth):
        zr, zf = _in_proj(h, g_mix, win, l, t["proj_rows"], hd, f_lo)
        casts = [(w, l) for w in layer_weights] + ([(w_in, l + 1)] if l + 1 < depth else [])
        b_out, wbp, wbh, wmo, wxq, wxkv, wxo, wfi, wfo, *nxt = _hgrn(
            zr, zf, hgrn_lower_bounds, g_hgrn, l, batch, seq, mall, lev, nlev, casts)
        win = nxt[0] if nxt else None
        h = _mix(zr, b_out, h, wg, sc_pool, wbp, wbh, wmo, l, batch, seq, t["mix_rows"])
        kv = _norm_mm(mem2, g_mem, wxkv, l, BF16, t["kv_rows"], t["kv_cols"])
        h = _xattn(h, g_cross, wxq, kv, wxo, l, batch, seq, mem_len, t["xattn_rows"])
        h = _ffn(h, g_ffn, wfi, wfo, l, t["ffn_rows"], t["ffn_cols"], g_final if l == depth - 1 else None)
    return h.reshape(batch, seq, d)
```

```python
import functools

import numpy as np
import jax
import jax.numpy as jnp
from jax import lax
from jax.experimental import pallas as pl
from jax.experimental.pallas import tpu as pltpu

F32 = jnp.float32
BF16 = jnp.bfloat16
EPS = 1e-6
LOG2E = 1.4426950408889634

SUBLANES = 8
LANES = 128
BF16_ROW_TILE = 16

POOL_WINDOWS = (2, 4, 8, 16)
POOL_HALO = 16
MIX_SLABS = 2
MIX_COL_CHUNKS = 4
XATTN_SLABS = 2
HGRN_HEAD = 128
HGRN_CHUNK = 128
X_HEADS = 4

VMEM_LIMIT = 56 * 1024 * 1024


def _params(sem):
    return pltpu.CompilerParams(dimension_semantics=sem, vmem_limit_bytes=VMEM_LIMIT)


def _resident(block_shape, index_map):
    return pl.BlockSpec(block_shape, index_map, pipeline_mode=pl.Buffered(1))


def _rmsnorm(x, gain):
    ms = jnp.mean(x * x, axis=-1, keepdims=True)
    return x * lax.rsqrt(ms + EPS) * gain


def _dot(a, b):
    return jnp.dot(a, b, preferred_element_type=F32)


def _dot_nt(a, b):
    return lax.dot_general(a, b, (((1,), (1,)), ((), ())), preferred_element_type=F32)


def _dot_tn(a, b):
    return lax.dot_general(a, b, (((0,), (0,)), ((), ())), preferred_element_type=F32)


def _sigmoid(x):
    return 1.0 / (1.0 + jnp.exp(-x))


def _norm_mm_kernel(x_ref, g_ref, w_ref, o_ref, u_ref):
    @pl.when(pl.program_id(1) == 0)
    def _():
        u_ref[...] = _rmsnorm(x_ref[...], g_ref[...]).astype(BF16)

    o_ref[...] = _dot(u_ref[...], w_ref[...]).astype(o_ref.dtype)


def _norm_mm(x, gain_all, w, layer, out_dtype, tm, tn):
    m, d = x.shape
    n = w.shape[-1]
    return pl.pallas_call(
        _norm_mm_kernel,
        grid=(m // tm, n // tn),
        in_specs=[
            pl.BlockSpec((tm, d), lambda i, j: (i, 0)),
            pl.BlockSpec((None, 1, d), lambda i, j: (layer, 0, 0)),
            pl.BlockSpec((d, tn), lambda i, j: (0, j)),
        ],
        out_specs=pl.BlockSpec((tm, tn), lambda i, j: (i, j)),
        out_shape=jax.ShapeDtypeStruct((m, n), out_dtype),
        scratch_shapes=[pltpu.VMEM((tm, d), BF16)],
        compiler_params=_params(("parallel", "arbitrary")),
        name="norm_mm",
    )(x, gain_all, w)


def _in_proj_kernel(x_ref, g_ref, w_ref, zr_ref, zf_ref, u_ref):
    @pl.when(pl.program_id(1) == 0)
    def _():
        u_ref[...] = _rmsnorm(x_ref[...], g_ref[...]).astype(BF16)

    z = _dot(u_ref[...], w_ref[...])
    zr_ref[...] = z.astype(BF16)
    zf_ref[...] = z


def _in_proj(x, gain_all, w, layer, tm, tn, f_lo):
    m, d = x.shape
    n = w.shape[-1]
    nb = n // tn
    f_blk = f_lo // tn

    def w_col(j):
        return jnp.where(j == nb - 1, f_blk, j + (j >= f_blk).astype(jnp.int32))

    return pl.pallas_call(
        _in_proj_kernel,
        grid=(m // tm, nb),
        in_specs=[
            pl.BlockSpec((tm, d), lambda i, j: (i, 0)),
            pl.BlockSpec((None, 1, d), lambda i, j: (layer, 0, 0)),
            pl.BlockSpec((d, tn), lambda i, j: (0, w_col(j))),
        ],
        out_specs=[
            pl.BlockSpec((tm, tn), lambda i, j: (i, j)),
            pl.BlockSpec((tm, tn), lambda i, j: (i, 0)),
        ],
        out_shape=[jax.ShapeDtypeStruct((m, n), BF16), jax.ShapeDtypeStruct((m, tn), F32)],
        scratch_shapes=[pltpu.VMEM((tm, d), BF16)],
        compiler_params=_params(("parallel", "arbitrary")),
        name="in_proj",
    )(x, gain_all, w)


def _hgrn_tables(c):
    nlev = int(np.log2(c))
    t = np.arange(c)[:, None]
    j = np.arange(c)[None, :]
    blocks = [(j <= t)]
    for l in range(1, nlev + 1):
        m = 2 ** (l - 1)
        r = (t // (2 * m)) * (2 * m) + m - 1
        right = (t % (2 * m)) >= m
        blocks.append(np.where(right, (j > r) & (j <= t), (j > t) & (j <= r)))
    blocks.append(j > t)
    mall = np.concatenate(blocks, axis=0).astype(np.float32)
    mall = np.concatenate([mall, mall], axis=1)
    x = t ^ j
    lev = np.zeros((c, c), np.int32)
    low = j < t
    lev[low] = np.floor(np.log2(x[low])).astype(np.int32) + 1
    lev[np.arange(c), np.arange(c)] = nlev + 1
    return mall, lev, nlev


def _silu(z):
    return z / (1.0 + jnp.exp(-z))


def _hgrn_kernel(lb_ref, q_ref, f_ref, i_ref, og_ref, gain_ref, mall_ref, lev_ref, *rest,
                 layer, nlev, heads, n_cast):
    cast_src = rest[:n_cast]
    o_ref = rest[n_cast]
    cast_dst = rest[n_cast + 1:2 * n_cast + 1]
    state_ref, lbs_ref, w_scr = rest[2 * n_cast + 1:]
    c = HGRN_CHUNK
    hw = HGRN_HEAD
    pw = 2 * hw

    @pl.when(pl.program_id(1) == 0)
    def _():
        state_ref[...] = jnp.zeros_like(state_ref)
        x = lb_ref[...]
        e = jnp.exp(x - jnp.max(x, axis=0, keepdims=True))
        sm = e / jnp.sum(e, axis=0, keepdims=True)
        lb = jnp.zeros_like(sm[0:1])
        for r in range(1, layer + 1):
            lb = lb + sm[r:r + 1]
        lbs_ref[0:1, :] = jnp.log2(lb)
        lbs_ref[1:2, :] = jnp.log2(1.0 - lb)
        lbs_ref[2:3, :] = 1.0 - lb

    sub = SUBLANES
    nblk = c // sub
    lev = lev_ref[...]
    lev_blk = [lev[i * sub:(i + 1) * sub] for i in range(nblk)]
    row_in_blk = lax.broadcasted_iota(jnp.int32, (sub, hw), 0)

    def is_right(row, m):
        return (row & m) != 0

    def level_operand(q_rows, k_rows, w_l, m):
        if m >= sub:
            qk = [q_rows[i] if is_right(i * sub, m) else k_rows[i] for i in range(nblk)]
        else:
            pick_q = is_right(row_in_blk, m)
            qk = [jnp.where(pick_q, q_rows[i], k_rows[i]) for i in range(nblk)]
        return (jnp.concatenate(qk, axis=0) * w_l).astype(BF16)

    def one_head(q_h, k_h, w_h, v, hidx):
        q_rows = [q_h[i * sub:(i + 1) * sub] for i in range(nblk)]
        k_rows = [k_h[i * sub:(i + 1) * sub] for i in range(nblk)]
        s0 = _dot_nt(q_h.astype(BF16), k_h.astype(BF16))
        s_lv = []
        for l in range(1, nlev + 1):
            x = level_operand(q_rows, k_rows, w_h[l * c:(l + 1) * c], 2 ** (l - 1))
            s_lv.append(_dot_nt(x, x))
        s_rows = [jnp.where(lev_blk[i] == nlev + 1, s0[i * sub:(i + 1) * sub], 0.0) for i in range(nblk)]
        for l in range(1, nlev + 1):
            m = 2 ** (l - 1)
            s_l = s_lv[l - 1]
            for i in range(nblk):
                if m < sub or is_right(i * sub, m):
                    s_rows[i] = jnp.where(lev_blk[i] == l, s_l[i * sub:(i + 1) * sub], s_rows[i])
        s = jnp.concatenate(s_rows, axis=0).astype(BF16)
        state_t = state_ref[hidx]
        o = _dot(s, v) + _dot_nt((q_h * w_h[0:c]).astype(BF16), state_t.astype(BF16))
        k_dec = (k_h * w_h[(nlev + 1) * c:(nlev + 2) * c]).astype(BF16)
        state_ref[hidx] = state_t * w_h[c - 1:c] + _dot_tn(v, k_dec)
        return o * lax.rsqrt(jnp.mean(o * o, axis=1, keepdims=True) + EPS)

    def gates(hp):
        ps = slice(hp * pw, (hp + 1) * pw)
        q = _silu(q_ref[:, ps]).astype(F32)
        log2_lb = lbs_ref[0:1, ps]
        log2_1m_lb = lbs_ref[1:2, ps]
        one_m_lb = lbs_ref[2:3, ps]
        z2 = f_ref[:, ps] * LOG2E
        e = jnp.exp2(-jnp.abs(z2))
        ope = 1.0 + e
        rcp = 1.0 / ope
        k = one_m_lb * jnp.where(z2 >= 0, e * rcp, rcp)
        cc = log2_1m_lb + (jnp.minimum(z2, 0.0) - jnp.log2(ope))
        g = jnp.maximum(log2_lb, cc) + jnp.log2(1.0 + jnp.exp2(-jnp.abs(log2_lb - cc)))
        g_hi = g.astype(BF16)
        g_lo = (g - g_hi.astype(F32)).astype(BF16)
        w_scr[hp % 2] = jnp.exp2(_dot(mall_ref[...], jnp.concatenate([g_hi, g_lo], axis=0)))
        return q, k

    nxt = gates(0)
    for hp in range(heads // 2):
        ps = slice(hp * pw, (hp + 1) * pw)
        q, k = nxt
        outs = []
        for j in range(2):
            cols = slice(j * hw, (j + 1) * hw)
            v = i_ref[:, hp * pw + j * hw:hp * pw + (j + 1) * hw]
            outs.append(one_head(q[:, cols], k[:, cols], w_scr.at[hp % 2, :, cols], v, 2 * hp + j))
            if j == 0 and hp + 1 < heads // 2:
                nxt = gates(hp + 1)
        o = (jnp.concatenate(outs, axis=1) * gain_ref[:, ps]).astype(BF16)
        o_ref[:, ps] = o * _silu(og_ref[:, ps])
        for src, dst in list(zip(cast_src, cast_dst))[hp::heads // 2]:
            dst[...] = src[...].astype(BF16)


def _cast_plan(w, layer, steps):
    _, r, cols = w.shape
    csplit = 1
    while r % ((steps // csplit) * BF16_ROW_TILE) != 0:
        csplit *= 2
    assert steps % csplit == 0 and cols % (csplit * LANES) == 0, (w.shape, steps)
    br, bc = r // (steps // csplit), cols // csplit
    in_spec = lambda step: pl.BlockSpec((None, br, bc), lambda b, t: (layer, step(b, t) // csplit, step(b, t) % csplit))
    out_spec = lambda step: pl.BlockSpec((br, bc), lambda b, t: (step(b, t) // csplit, step(b, t) % csplit))
    return in_spec, out_spec, jax.ShapeDtypeStruct((r, cols), BF16)


def _hgrn(zr, zf, lb_raw, gain_all, layer, batch, seq, mall, lev, nlev, casts):
    m = zr.shape[0]
    hd = zf.shape[1]
    depth = lb_raw.shape[0]
    c = HGRN_CHUNK
    nc = seq // c
    heads = hd // HGRN_HEAD
    row = lambda b, t: b * nc + t
    plans = [_cast_plan(w, wl, batch * nc) for w, wl in casts]
    return pl.pallas_call(
        functools.partial(_hgrn_kernel, layer=layer, nlev=nlev, heads=heads, n_cast=len(casts)),
        grid=(batch, nc),
        in_specs=[
            _resident((depth, hd), lambda b, t: (0, 0)),
            pl.BlockSpec((c, hd), lambda b, t: (row(b, t), 1)),
            pl.BlockSpec((c, hd), lambda b, t: (row(b, t), 0)),
            pl.BlockSpec((c, hd), lambda b, t: (row(b, t), 2)),
            pl.BlockSpec((c, hd), lambda b, t: (row(b, t), 3)),
            pl.BlockSpec((None, 1, hd), lambda b, t: (layer, 0, 0)),
            _resident(mall.shape, lambda b, t: (0, 0)),
            _resident(lev.shape, lambda b, t: (0, 0)),
        ] + [p[0](row) for p in plans],
        out_specs=[pl.BlockSpec((c, hd), lambda b, t: (row(b, t), 0))] + [p[1](row) for p in plans],
        out_shape=[jax.ShapeDtypeStruct((m, hd), BF16)] + [p[2] for p in plans],
        scratch_shapes=[pltpu.VMEM((heads, HGRN_HEAD, HGRN_HEAD), F32),
                        pltpu.VMEM((8, hd), F32),
                        pltpu.VMEM((2, mall.shape[0], 2 * HGRN_HEAD), F32)],
        compiler_params=_params(("parallel", "arbitrary")),
        name="hgrn",
    )(lb_raw, zr, zf, zr, zr, gain_all, mall, lev, *[w for w, _ in casts])


def _mix_kernel(zp_ref, halo_ref, ga_ref, gb_ref, bo_ref, h_ref, wg_ref, sc_ref,
                wbp_ref, wbh_ref, wmo_ref, o_ref, *, tm):
    rows = tm // MIX_SLABS
    gd = zp_ref.shape[1] // len(POOL_WINDOWS)
    for part in range(MIX_SLABS):
        rs = slice(part * rows, (part + 1) * rows)
        t0 = pl.program_id(1) * tm + part * rows
        x = zp_ref[rs, :].astype(F32)
        if part == 0:
            halo = jnp.where(t0 > 0, halo_ref[...].astype(F32), 0.0)
        else:
            halo = zp_ref[part * rows - POOL_HALO:part * rows, :].astype(F32)
        ext = jnp.concatenate([halo, x], axis=0)
        pos = lax.broadcasted_iota(jnp.int32, (rows, 1), 0) + t0
        outs = []
        for g, wdw in enumerate(POOL_WINDOWS):
            cols = slice(g * gd, (g + 1) * gd)
            e = ext[:, cols]
            sh = 1
            while sh < wdw:
                e = e + pltpu.roll(e, sh, axis=0)
                sh *= 2
            cnt = jnp.minimum(pos + 1, wdw).astype(F32)
            pooled = (e[POOL_HALO:] / cnt - x[:, cols]).astype(BF16)
            outs.append((_dot(pooled, wg_ref[g]) * sc_ref[:, cols]).astype(BF16))
        a_out = jnp.concatenate(outs, axis=1)
        d = o_ref.shape[1]
        cw = d // MIX_COL_CHUNKS
        chunks = []
        for cc in range(MIX_COL_CHUNKS):
            cs = slice(cc * cw, (cc + 1) * cw)
            part_sum = _sigmoid(ga_ref[rs, cs].astype(F32)) * _dot(a_out, wbp_ref[:, cs])
            part_sum = part_sum + _sigmoid(gb_ref[rs, cs].astype(F32)) * _dot(bo_ref[rs, :], wbh_ref[:, cs])
            chunks.append(part_sum.astype(BF16))
        merged = jnp.concatenate(chunks, axis=1)
        o_ref[rs, :] = h_ref[rs, :] + _dot(merged, wmo_ref[...])


def _mix(zr, b_out, h, wg_all, sc_all, wbp, wbh, wmo, layer, batch, seq, tm):
    m, d = h.shape
    pd = b_out.shape[1]
    ng, gd = wg_all.shape[1], wg_all.shape[2]
    nt = seq // tm
    row = lambda b, t: b * nt + t
    hb = tm // POOL_HALO
    return pl.pallas_call(
        functools.partial(_mix_kernel, tm=tm),
        grid=(batch, nt),
        in_specs=[
            pl.BlockSpec((tm, pd), lambda b, t: (row(b, t), 0)),
            pl.BlockSpec((POOL_HALO, pd), lambda b, t: (jnp.maximum(row(b, t) * hb - 1, 0), 0)),
            pl.BlockSpec((tm, d), lambda b, t: (row(b, t), 2)),
            pl.BlockSpec((tm, d), lambda b, t: (row(b, t), 3)),
            pl.BlockSpec((tm, pd), lambda b, t: (row(b, t), 0)),
            pl.BlockSpec((tm, d), lambda b, t: (row(b, t), 0)),
            _resident((None, ng, gd, gd), lambda b, t: (layer, 0, 0, 0)),
            _resident((None, 1, pd), lambda b, t: (layer, 0, 0)),
            _resident((pd, d), lambda b, t: (0, 0)),
            _resident((pd, d), lambda b, t: (0, 0)),
            _resident((d, d), lambda b, t: (0, 0)),
        ],
        out_specs=pl.BlockSpec((tm, d), lambda b, t: (row(b, t), 0)),
        out_shape=jax.ShapeDtypeStruct((m, d), F32),
        compiler_params=_params(("parallel", "parallel")),
        name="mix",
    )(zr, zr, zr, zr, b_out, h, wg_all, sc_all, wbp, wbh, wmo)


def _xattn_kernel(h_ref, g_ref, wq_ref, kv_ref, wo_ref, o_ref, att_ref):
    h = h_ref[...]
    d = h.shape[1]
    dh = d // X_HEADS
    rows = h.shape[0] // XATTN_SLABS
    q = jnp.concatenate(
        [_dot(_rmsnorm(h[r * rows:(r + 1) * rows], g_ref[...]).astype(BF16), wq_ref[...]).astype(BF16)
         for r in range(XATTN_SLABS)], axis=0)
    for hh in range(X_HEADS):
        cols = slice(hh * dh, (hh + 1) * dh)
        s = _dot_nt(q[:, cols], kv_ref[:, cols]) * (dh ** -0.5)
        p = jnp.exp(s - jnp.max(s, axis=-1, keepdims=True))
        p = p / jnp.sum(p, axis=-1, keepdims=True)
        att_ref[:, cols] = _dot(p.astype(BF16), kv_ref[:, d + hh * dh:d + (hh + 1) * dh]).astype(BF16)
    o_ref[...] = h + _dot(att_ref[...], wo_ref[...])


def _xattn(h, gain_all, wq, kv, wo, layer, batch, seq, mem_len, tm):
    m, d = h.shape
    nt = seq // tm
    row = lambda b, t: b * nt + t
    return pl.pallas_call(
        _xattn_kernel,
        grid=(batch, nt),
        in_specs=[
            pl.BlockSpec((tm, d), lambda b, t: (row(b, t), 0)),
            pl.BlockSpec((None, 1, d), lambda b, t: (layer, 0, 0)),
            _resident((d, d), lambda b, t: (0, 0)),
            pl.BlockSpec((mem_len, 2 * d), lambda b, t: (b, 0)),
            _resident((d, d), lambda b, t: (0, 0)),
        ],
        out_specs=pl.BlockSpec((tm, d), lambda b, t: (row(b, t), 0)),
        out_shape=jax.ShapeDtypeStruct((m, d), F32),
        scratch_shapes=[pltpu.VMEM((tm, d), BF16)],
        compiler_params=_params(("parallel", "parallel")),
        name="xattn",
    )(h, gain_all, wq, kv, wo)


def _ffn_kernel(h_ref, g_ref, wg_ref, wu_ref, wo_ref, *rest, final):
    if final:
        gf_ref, o_ref, u_ref = rest
    else:
        o_ref, u_ref = rest
    f = pl.program_id(1)

    def partial_sum():
        u = u_ref[...]
        gate = _dot(u, wg_ref[...])
        up = _dot(u, wu_ref[...])
        act = (gate * _sigmoid(gate) * up).astype(BF16)
        return _dot(act, wo_ref[...])

    @pl.when(f == 0)
    def _():
        h = h_ref[...]
        u_ref[...] = _rmsnorm(h, g_ref[...]).astype(BF16)
        o_ref[...] = h + partial_sum()

    @pl.when(f > 0)
    def _():
        o_ref[...] += partial_sum()

    if final:
        @pl.when(f == pl.num_programs(1) - 1)
        def _():
            o_ref[...] = _rmsnorm(o_ref[...], gf_ref[...])


def _ffn(h, gain_all, w_in, w_out, layer, tm, tf, final_gain=None):
    m, d = h.shape
    dff = w_out.shape[0]
    nf = dff // tf
    final = final_gain is not None
    in_specs = [
        pl.BlockSpec((tm, d), lambda i, f: (i, 0)),
        pl.BlockSpec((None, 1, d), lambda i, f: (layer, 0, 0)),
        pl.BlockSpec((d, tf), lambda i, f: (0, f)),
        pl.BlockSpec((d, tf), lambda i, f: (0, nf + f)),
        pl.BlockSpec((tf, d), lambda i, f: (f, 0)),
    ]
    args = [h, gain_all, w_in, w_in, w_out]
    if final:
        in_specs.append(pl.BlockSpec((1, d), lambda i, f: (0, 0)))
        args.append(final_gain)
    return pl.pallas_call(
        functools.partial(_ffn_kernel, final=final),
        grid=(m // tm, nf),
        in_specs=in_specs,
        out_specs=pl.BlockSpec((tm, d), lambda i, f: (i, 0)),
        out_shape=jax.ShapeDtypeStruct((m, d), F32),
        scratch_shapes=[pltpu.VMEM((tm, d), BF16)],
        compiler_params=_params(("parallel", "arbitrary")),
        name="ffn",
    )(*args)


def _tiles(tokens, seq, mem_tokens, hd):
    return {
        "proj_rows": min(1024, tokens),
        "mix_rows": min(512, seq),
        "kv_rows": min(1024, mem_tokens),
        "kv_cols": hd,
        "xattn_rows": min(512, seq),
        "ffn_rows": min(1024, tokens),
        "ffn_cols": 512,
    }


def kernel(x, mem, w_in, w_pool_group, pool_scale, hgrn_lower_bounds, hgrn_norm, w_branch_pool,
           w_branch_hgrn, w_mix_out, norm_mix, norm_mem, norm_cross, w_xq, w_xkv, w_xo, norm_ffn,
           w_ffn_in, w_ffn_out, norm_final):
    batch, seq, d = x.shape
    mem_len = mem.shape[1]
    depth = w_in.shape[0]
    pd = w_branch_pool.shape[1]
    hd = hgrn_lower_bounds.shape[1]

    f_lo = pd + hd
    win = w_in[0].astype(BF16)
    wg = w_pool_group.astype(BF16)
    layer_weights = (w_branch_pool, w_branch_hgrn, w_mix_out, w_xq, w_xkv, w_xo, w_ffn_in, w_ffn_out)

    as_rows = lambda a: a.reshape(a.shape[0], 1, a.shape[1])
    g_mix, g_mem, g_cross, g_ffn = map(as_rows, (norm_mix, norm_mem, norm_cross, norm_ffn))
    g_hgrn, sc_pool = as_rows(hgrn_norm), as_rows(pool_scale)
    g_final = norm_final.reshape(1, d)

    mall_np, lev_np, nlev = _hgrn_tables(HGRN_CHUNK)
    mall = jnp.asarray(mall_np, BF16)
    lev = jnp.asarray(lev_np)

    t = _tiles(batch * seq, seq, batch * mem_len, hd)
    h = x.reshape(batch * seq, d)
    mem2 = mem.reshape(batch * mem_len, d)
    for l in range(depth):
        zr, zf = _in_proj(h, g_mix, win, l, t["proj_rows"], hd, f_lo)
        casts = [(w, l) for w in layer_weights] + ([(w_in, l + 1)] if l + 1 < depth else [])
        b_out, wbp, wbh, wmo, wxq, wxkv, wxo, wfi, wfo, *nxt = _hgrn(
            zr, zf, hgrn_lower_bounds, g_hgrn, l, batch, seq, mall, lev, nlev, casts)
        win = nxt[0] if nxt else None
        h = _mix(zr, b_out, h, wg, sc_pool, wbp, wbh, wmo, l, batch, seq, t["mix_rows"])
        kv = _norm_mm(mem2, g_mem, wxkv, l, BF16, t["kv_rows"], t["kv_cols"])
        h = _xattn(h, g_cross, wxq, kv, wxo, l, batch, seq, mem_len, t["xattn_rows"])
        h = _ffn(h, g_ffn, wfi, wfo, l, t["ffn_rows"], t["ffn_cols"], g_final if l == depth - 1 else None)
    return h.reshape(batch, seq, d)
```

```python
import functools

import numpy as np
import jax
import jax.numpy as jnp
from jax import lax
from jax.experimental import pallas as pl
from jax.experimental.pallas import tpu as pltpu

F32 = jnp.float32
BF16 = jnp.bfloat16
EPS = 1e-6
LOG2E = 1.4426950408889634

SUBLANES = 8
LANES = 128
BF16_ROW_TILE = 16

POOL_WINDOWS = (2, 4, 8, 16)
POOL_HALO = 16
MIX_SLABS = 2
MIX_COL_CHUNKS = 4
XATTN_SLABS = 2
FFN_COL_CHUNKS = 2
PROJ_NORM_SLABS = 4
HGRN_HEAD = 128
HGRN_CHUNK = 128
X_HEADS = 4

VMEM_LIMIT = 56 * 1024 * 1024


def _params(sem):
    return pltpu.CompilerParams(dimension_semantics=sem, vmem_limit_bytes=VMEM_LIMIT)


def _resident(block_shape, index_map):
    return pl.BlockSpec(block_shape, index_map, pipeline_mode=pl.Buffered(1))


def _rmsnorm(x, gain):
    ms = jnp.mean(x * x, axis=-1, keepdims=True)
    return x * lax.rsqrt(ms + EPS) * gain


def _dot(a, b):
    return jnp.dot(a, b, preferred_element_type=F32)


def _dot_nt(a, b):
    return lax.dot_general(a, b, (((1,), (1,)), ((), ())), preferred_element_type=F32)


def _dot_tn(a, b):
    return lax.dot_general(a, b, (((0,), (0,)), ((), ())), preferred_element_type=F32)


def _sigmoid(x):
    return 1.0 / (1.0 + jnp.exp(-x))


def _norm_mm_kernel(x_ref, g_ref, w_ref, o_ref, u_ref):
    @pl.when(pl.program_id(1) == 0)
    def _():
        u_ref[...] = _rmsnorm(x_ref[...], g_ref[...]).astype(BF16)

    o_ref[...] = _dot(u_ref[...], w_ref[...]).astype(o_ref.dtype)


def _norm_mm(x, gain_all, w, layer, out_dtype, tm, tn):
    m, d = x.shape
    n = w.shape[-1]
    return pl.pallas_call(
        _norm_mm_kernel,
        grid=(m // tm, n // tn),
        in_specs=[
            pl.BlockSpec((tm, d), lambda i, j: (i, 0)),
            pl.BlockSpec((None, 1, d), lambda i, j: (layer, 0, 0)),
            pl.BlockSpec((d, tn), lambda i, j: (0, j)),
        ],
        out_specs=pl.BlockSpec((tm, tn), lambda i, j: (i, j)),
        out_shape=jax.ShapeDtypeStruct((m, n), out_dtype),
        scratch_shapes=[pltpu.VMEM((tm, d), BF16)],
        compiler_params=_params(("parallel", "arbitrary")),
        name="norm_mm",
    )(x, gain_all, w)


def _in_proj_kernel(x_ref, g_ref, w_ref, zr_ref, zf_ref, u_ref):
    j = pl.program_id(1)

    @pl.when(j == 0)
    def _():
        rows = x_ref.shape[0] // PROJ_NORM_SLABS
        for r in range(PROJ_NORM_SLABS):
            rs = slice(r * rows, (r + 1) * rows)
            u = _rmsnorm(x_ref[rs, :], g_ref[...]).astype(BF16)
            u_ref[rs, :] = u
            z = _dot(u, w_ref[...])
            zr_ref[rs, :] = z.astype(BF16)
            zf_ref[rs, :] = z

    @pl.when(j > 0)
    def _():
        z = _dot(u_ref[...], w_ref[...])
        zr_ref[...] = z.astype(BF16)
        zf_ref[...] = z


def _in_proj(x, gain_all, w, layer, tm, tn, f_lo):
    m, d = x.shape
    n = w.shape[-1]
    nb = n // tn
    f_blk = f_lo // tn

    def w_col(j):
        return jnp.where(j == nb - 1, f_blk, j + (j >= f_blk).astype(jnp.int32))

    return pl.pallas_call(
        _in_proj_kernel,
        grid=(m // tm, nb),
        in_specs=[
            pl.BlockSpec((tm, d), lambda i, j: (i, 0)),
            pl.BlockSpec((None, 1, d), lambda i, j: (layer, 0, 0)),
            pl.BlockSpec((d, tn), lambda i, j: (0, w_col(j))),
        ],
        out_specs=[
            pl.BlockSpec((tm, tn), lambda i, j: (i, j)),
            pl.BlockSpec((tm, tn), lambda i, j: (i, 0)),
        ],
        out_shape=[jax.ShapeDtypeStruct((m, n), BF16), jax.ShapeDtypeStruct((m, tn), F32)],
        scratch_shapes=[pltpu.VMEM((tm, d), BF16)],
        compiler_params=_params(("parallel", "arbitrary")),
        name="in_proj",
    )(x, gain_all, w)


def _hgrn_tables(c):
    nlev = int(np.log2(c))
    t = np.arange(c)[:, None]
    j = np.arange(c)[None, :]
    blocks = [(j <= t)]
    for l in range(1, nlev + 1):
        m = 2 ** (l - 1)
        r = (t // (2 * m)) * (2 * m) + m - 1
        right = (t % (2 * m)) >= m
        blocks.append(np.where(right, (j > r) & (j <= t), (j > t) & (j <= r)))
    blocks.append(j > t)
    mall = np.concatenate(blocks, axis=0).astype(np.float32)
    mall = np.concatenate([mall, mall], axis=1)
    x = t ^ j
    lev = np.zeros((c, c), np.int32)
    low = j < t
    lev[low] = np.floor(np.log2(x[low])).astype(np.int32) + 1
    lev[np.arange(c), np.arange(c)] = nlev + 1
    return mall, lev, nlev


def _silu(z):
    return z / (1.0 + jnp.exp(-z))


def _hgrn_kernel(lb_ref, q_ref, f_ref, i_ref, og_ref, gain_ref, mall_ref, lev_ref, *rest,
                 layer, nlev, heads, n_cast):
    cast_src = rest[:n_cast]
    o_ref = rest[n_cast]
    cast_dst = rest[n_cast + 1:2 * n_cast + 1]
    state_ref, lbs_ref, w_scr = rest[2 * n_cast + 1:]
    c = HGRN_CHUNK
    hw = HGRN_HEAD
    pw = 2 * hw

    @pl.when(pl.program_id(1) == 0)
    def _():
        state_ref[...] = jnp.zeros_like(state_ref)
        x = lb_ref[...]
        e = jnp.exp(x - jnp.max(x, axis=0, keepdims=True))
        sm = e / jnp.sum(e, axis=0, keepdims=True)
        lb = jnp.zeros_like(sm[0:1])
        for r in range(1, layer + 1):
            lb = lb + sm[r:r + 1]
        lbs_ref[0:1, :] = jnp.log2(lb)
        lbs_ref[1:2, :] = jnp.log2(1.0 - lb)
        lbs_ref[2:3, :] = 1.0 - lb

    sub = SUBLANES
    nblk = c // sub
    lev = lev_ref[...]
    lev_blk = [lev[i * sub:(i + 1) * sub] for i in range(nblk)]
    row_in_blk = lax.broadcasted_iota(jnp.int32, (sub, hw), 0)

    def is_right(row, m):
        return (row & m) != 0

    def level_operand(q_rows, k_rows, w_l, m):
        if m >= sub:
            qk = [q_rows[i] if is_right(i * sub, m) else k_rows[i] for i in range(nblk)]
        else:
            pick_q = is_right(row_in_blk, m)
            qk = [jnp.where(pick_q, q_rows[i], k_rows[i]) for i in range(nblk)]
        return (jnp.concatenate(qk, axis=0) * w_l).astype(BF16)

    def one_head(q_h, k_h, w_h, v, hidx):
        q_rows = [q_h[i * sub:(i + 1) * sub] for i in range(nblk)]
        k_rows = [k_h[i * sub:(i + 1) * sub] for i in range(nblk)]
        s0 = _dot_nt(q_h.astype(BF16), k_h.astype(BF16))
        s_lv = []
        for l in range(1, nlev + 1):
            x = level_operand(q_rows, k_rows, w_h[l * c:(l + 1) * c], 2 ** (l - 1))
            s_lv.append(_dot_nt(x, x))
        s_rows = [jnp.where(lev_blk[i] == nlev + 1, s0[i * sub:(i + 1) * sub], 0.0) for i in range(nblk)]
        for l in range(1, nlev + 1):
            m = 2 ** (l - 1)
            s_l = s_lv[l - 1]
            for i in range(nblk):
                if m < sub or is_right(i * sub, m):
                    s_rows[i] = jnp.where(lev_blk[i] == l, s_l[i * sub:(i + 1) * sub], s_rows[i])
        s = jnp.concatenate(s_rows, axis=0).astype(BF16)
        state_t = state_ref[hidx]
        o = _dot(s, v) + _dot_nt((q_h * w_h[0:c]).astype(BF16), state_t.astype(BF16))
        k_dec = (k_h * w_h[(nlev + 1) * c:(nlev + 2) * c]).astype(BF16)
        state_ref[hidx] = state_t * w_h[c - 1:c] + _dot_tn(v, k_dec)
        return o * lax.rsqrt(jnp.mean(o * o, axis=1, keepdims=True) + EPS)

    def gates(hp):
        ps = slice(hp * pw, (hp + 1) * pw)
        q = _silu(q_ref[:, ps]).astype(F32)
        log2_lb = lbs_ref[0:1, ps]
        log2_1m_lb = lbs_ref[1:2, ps]
        one_m_lb = lbs_ref[2:3, ps]
        z2 = f_ref[:, ps] * LOG2E
        e = jnp.exp2(-jnp.abs(z2))
        ope = 1.0 + e
        rcp = 1.0 / ope
        k = one_m_lb * jnp.where(z2 >= 0, e * rcp, rcp)
        cc = log2_1m_lb + (jnp.minimum(z2, 0.0) - jnp.log2(ope))
        g = jnp.maximum(log2_lb, cc) + jnp.log2(1.0 + jnp.exp2(-jnp.abs(log2_lb - cc)))
        g_hi = g.astype(BF16)
        g_lo = (g - g_hi.astype(F32)).astype(BF16)
        w_scr[hp % 2] = jnp.exp2(_dot(mall_ref[...], jnp.concatenate([g_hi, g_lo], axis=0)))
        return q, k

    nxt = gates(0)
    for hp in range(heads // 2):
        ps = slice(hp * pw, (hp + 1) * pw)
        q, k = nxt
        outs = []
        for j in range(2):
            cols = slice(j * hw, (j + 1) * hw)
            v = i_ref[:, hp * pw + j * hw:hp * pw + (j + 1) * hw]
            outs.append(one_head(q[:, cols], k[:, cols], w_scr.at[hp % 2, :, cols], v, 2 * hp + j))
            if j == 0 and hp + 1 < heads // 2:
                nxt = gates(hp + 1)
        o = (jnp.concatenate(outs, axis=1) * gain_ref[:, ps]).astype(BF16)
        o_ref[:, ps] = o * _silu(og_ref[:, ps])
        for src, dst in list(zip(cast_src, cast_dst))[hp::heads // 2]:
            dst[...] = src[...].astype(BF16)


def _cast_plan(w, layer, steps):
    _, r, cols = w.shape
    csplit = 1
    while r % ((steps // csplit) * BF16_ROW_TILE) != 0:
        csplit *= 2
    assert steps % csplit == 0 and cols % (csplit * LANES) == 0, (w.shape, steps)
    br, bc = r // (steps // csplit), cols // csplit
    in_spec = lambda step: pl.BlockSpec((None, br, bc), lambda b, t: (layer, step(b, t) // csplit, step(b, t) % csplit))
    out_spec = lambda step: pl.BlockSpec((br, bc), lambda b, t: (step(b, t) // csplit, step(b, t) % csplit))
    return in_spec, out_spec, jax.ShapeDtypeStruct((r, cols), BF16)


def _hgrn(zr, zf, lb_raw, gain_all, layer, batch, seq, mall, lev, nlev, casts):
    m = zr.shape[0]
    hd = zf.shape[1]
    depth = lb_raw.shape[0]
    c = HGRN_CHUNK
    nc = seq // c
    heads = hd // HGRN_HEAD
    row = lambda b, t: b * nc + t
    plans = [_cast_plan(w, wl, batch * nc) for w, wl in casts]
    return pl.pallas_call(
        functools.partial(_hgrn_kernel, layer=layer, nlev=nlev, heads=heads, n_cast=len(casts)),
        grid=(batch, nc),
        in_specs=[
            _resident((depth, hd), lambda b, t: (0, 0)),
            pl.BlockSpec((c, hd), lambda b, t: (row(b, t), 1)),
            pl.BlockSpec((c, hd), lambda b, t: (row(b, t), 0)),
            pl.BlockSpec((c, hd), lambda b, t: (row(b, t), 2)),
            pl.BlockSpec((c, hd), lambda b, t: (row(b, t), 3)),
            pl.BlockSpec((None, 1, hd), lambda b, t: (layer, 0, 0)),
            _resident(mall.shape, lambda b, t: (0, 0)),
            _resident(lev.shape, lambda b, t: (0, 0)),
        ] + [p[0](row) for p in plans],
        out_specs=[pl.BlockSpec((c, hd), lambda b, t: (row(b, t), 0))] + [p[1](row) for p in plans],
        out_shape=[jax.ShapeDtypeStruct((m, hd), BF16)] + [p[2] for p in plans],
        scratch_shapes=[pltpu.VMEM((heads, HGRN_HEAD, HGRN_HEAD), F32),
                        pltpu.VMEM((8, hd), F32),
                        pltpu.VMEM((2, mall.shape[0], 2 * HGRN_HEAD), F32)],
        compiler_params=_params(("parallel", "arbitrary")),
        name="hgrn",
    )(lb_raw, zr, zf, zr, zr, gain_all, mall, lev, *[w for w, _ in casts])


def _mix_kernel(zp_ref, halo_ref, ga_ref, gb_ref, bo_ref, h_ref, wg_ref, sc_ref,
                wbp_ref, wbh_ref, wmo_ref, o_ref, *, tm):
    rows = tm // MIX_SLABS
    gd = zp_ref.shape[1] // len(POOL_WINDOWS)
    for part in range(MIX_SLABS):
        rs = slice(part * rows, (part + 1) * rows)
        t0 = pl.program_id(1) * tm + part * rows
        x = zp_ref[rs, :].astype(F32)
        if part == 0:
            halo = jnp.where(t0 > 0, halo_ref[...].astype(F32), 0.0)
        else:
            halo = zp_ref[part * rows - POOL_HALO:part * rows, :].astype(F32)
        ext = jnp.concatenate([halo, x], axis=0)
        pos = lax.broadcasted_iota(jnp.int32, (rows, 1), 0) + t0
        outs = []
        for g, wdw in enumerate(POOL_WINDOWS):
            cols = slice(g * gd, (g + 1) * gd)
            e = ext[:, cols]
            sh = 1
            while sh < wdw:
                e = e + pltpu.roll(e, sh, axis=0)
                sh *= 2
            cnt = jnp.minimum(pos + 1, wdw).astype(F32)
            pooled = (e[POOL_HALO:] / cnt - x[:, cols]).astype(BF16)
            outs.append((_dot(pooled, wg_ref[g]) * sc_ref[:, cols]).astype(BF16))
        a_out = jnp.concatenate(outs, axis=1)
        d = o_ref.shape[1]
        cw = d // MIX_COL_CHUNKS
        chunks = []
        for cc in range(MIX_COL_CHUNKS):
            cs = slice(cc * cw, (cc + 1) * cw)
            part_sum = _sigmoid(ga_ref[rs, cs].astype(F32)) * _dot(a_out, wbp_ref[:, cs])
            part_sum = part_sum + _sigmoid(gb_ref[rs, cs].astype(F32)) * _dot(bo_ref[rs, :], wbh_ref[:, cs])
            chunks.append(part_sum.astype(BF16))
        merged = jnp.concatenate(chunks, axis=1)
        o_ref[rs, :] = h_ref[rs, :] + _dot(merged, wmo_ref[...])


def _mix(zr, b_out, h, wg_all, sc_all, wbp, wbh, wmo, layer, batch, seq, tm):
    m, d = h.shape
    pd = b_out.shape[1]
    ng, gd = wg_all.shape[1], wg_all.shape[2]
    nt = seq // tm
    row = lambda b, t: b * nt + t
    hb = tm // POOL_HALO
    return pl.pallas_call(
        functools.partial(_mix_kernel, tm=tm),
        grid=(batch, nt),
        in_specs=[
            pl.BlockSpec((tm, pd), lambda b, t: (row(b, t), 0)),
            pl.BlockSpec((POOL_HALO, pd), lambda b, t: (jnp.maximum(row(b, t) * hb - 1, 0), 0)),
            pl.BlockSpec((tm, d), lambda b, t: (row(b, t), 2)),
            pl.BlockSpec((tm, d), lambda b, t: (row(b, t), 3)),
            pl.BlockSpec((tm, pd), lambda b, t: (row(b, t), 0)),
            pl.BlockSpec((tm, d), lambda b, t: (row(b, t), 0)),
            _resident((None, ng, gd, gd), lambda b, t: (layer, 0, 0, 0)),
            _resident((None, 1, pd), lambda b, t: (layer, 0, 0)),
            _resident((pd, d), lambda b, t: (0, 0)),
            _resident((pd, d), lambda b, t: (0, 0)),
            _resident((d, d), lambda b, t: (0, 0)),
        ],
        out_specs=pl.BlockSpec((tm, d), lambda b, t: (row(b, t), 0)),
        out_shape=jax.ShapeDtypeStruct((m, d), F32),
        compiler_params=_params(("parallel", "parallel")),
        name="mix",
    )(zr, zr, zr, zr, b_out, h, wg_all, sc_all, wbp, wbh, wmo)


def _xattn_kernel(h_ref, g_ref, wq_ref, kv_ref, wo_ref, o_ref, att_ref):
    h = h_ref[...]
    d = h.shape[1]
    dh = d // X_HEADS
    rows = h.shape[0] // XATTN_SLABS
    q = jnp.concatenate(
        [_dot(_rmsnorm(h[r * rows:(r + 1) * rows], g_ref[...]).astype(BF16), wq_ref[...]).astype(BF16)
         for r in range(XATTN_SLABS)], axis=0)
    for hh in range(X_HEADS):
        cols = slice(hh * dh, (hh + 1) * dh)
        s = _dot_nt(q[:, cols], kv_ref[:, cols]) * (dh ** -0.5)
        p = jnp.exp(s - jnp.max(s, axis=-1, keepdims=True))
        p = p / jnp.sum(p, axis=-1, keepdims=True)
        att_ref[:, cols] = _dot(p.astype(BF16), kv_ref[:, d + hh * dh:d + (hh + 1) * dh]).astype(BF16)
    o_ref[...] = h + _dot(att_ref[...], wo_ref[...])


def _xattn(h, gain_all, wq, kv, wo, layer, batch, seq, mem_len, tm):
    m, d = h.shape
    nt = seq // tm
    row = lambda b, t: b * nt + t
    return pl.pallas_call(
        _xattn_kernel,
        grid=(batch, nt),
        in_specs=[
            pl.BlockSpec((tm, d), lambda b, t: (row(b, t), 0)),
            pl.BlockSpec((None, 1, d), lambda b, t: (layer, 0, 0)),
            _resident((d, d), lambda b, t: (0, 0)),
            pl.BlockSpec((mem_len, 2 * d), lambda b, t: (b, 0)),
            _resident((d, d), lambda b, t: (0, 0)),
        ],
        out_specs=pl.BlockSpec((tm, d), lambda b, t: (row(b, t), 0)),
        out_shape=jax.ShapeDtypeStruct((m, d), F32),
        scratch_shapes=[pltpu.VMEM((tm, d), BF16)],
        compiler_params=_params(("parallel", "parallel")),
        name="xattn",
    )(h, gain_all, wq, kv, wo)


def _ffn_kernel(h_ref, g_ref, wg_ref, wu_ref, wo_ref, *rest, final):
    if final:
        gf_ref, o_ref, u_ref = rest
    else:
        o_ref, u_ref = rest
    f = pl.program_id(1)

    def partial_sum(u):
        tf = wg_ref.shape[1]
        cw = tf // FFN_COL_CHUNKS
        acts = []
        for cc in range(FFN_COL_CHUNKS):
            cs = slice(cc * cw, (cc + 1) * cw)
            gate = _dot(u, wg_ref[:, cs])
            up = _dot(u, wu_ref[:, cs])
            acts.append((gate * _sigmoid(gate) * up).astype(BF16))
        return _dot(jnp.concatenate(acts, axis=1), wo_ref[...])

    @pl.when(f == 0)
    def _():
        h = h_ref[...]
        u = _rmsnorm(h, g_ref[...]).astype(BF16)
        u_ref[...] = u
        o_ref[...] = h + partial_sum(u)

    @pl.when(f > 0)
    def _():
        o_ref[...] += partial_sum(u_ref[...])

    if final:
        @pl.when(f == pl.num_programs(1) - 1)
        def _():
            o_ref[...] = _rmsnorm(o_ref[...], gf_ref[...])


def _ffn(h, gain_all, w_in, w_out, layer, tm, tf, final_gain=None):
    m, d = h.shape
    dff = w_out.shape[0]
    nf = dff // tf
    final = final_gain is not None
    in_specs = [
        pl.BlockSpec((tm, d), lambda i, f: (i, 0)),
        pl.BlockSpec((None, 1, d), lambda i, f: (layer, 0, 0)),
        pl.BlockSpec((d, tf), lambda i, f: (0, f)),
        pl.BlockSpec((d, tf), lambda i, f: (0, nf + f)),
        pl.BlockSpec((tf, d), lambda i, f: (f, 0)),
    ]
    args = [h, gain_all, w_in, w_in, w_out]
    if final:
        in_specs.append(pl.BlockSpec((1, d), lambda i, f: (0, 0)))
        args.append(final_gain)
    return pl.pallas_call(
        functools.partial(_ffn_kernel, final=final),
        grid=(m // tm, nf),
        in_specs=in_specs,
        out_specs=pl.BlockSpec((tm, d), lambda i, f: (i, 0)),
        out_shape=jax.ShapeDtypeStruct((m, d), F32),
        scratch_shapes=[pltpu.VMEM((tm, d), BF16)],
        compiler_params=_params(("parallel", "arbitrary")),
        name="ffn",
    )(*args)


def _tiles(tokens, seq, mem_tokens, hd):
    return {
        "proj_rows": min(1024, tokens),
        "mix_rows": min(512, seq),
        "kv_rows": min(1024, mem_tokens),
        "kv_cols": hd,
        "xattn_rows": min(512, seq),
        "ffn_rows": min(1024, tokens),
        "ffn_cols": 512,
    }


def kernel(x, mem, w_in, w_pool_group, pool_scale, hgrn_lower_bounds, hgrn_norm, w_branch_pool,
           w_branch_hgrn, w_mix_out, norm_mix, norm_mem, norm_cross, w_xq, w_xkv, w_xo, norm_ffn,
           w_ffn_in, w_ffn_out, norm_final):
    batch, seq, d = x.shape
    mem_len = mem.shape[1]
    depth = w_in.shape[0]
    pd = w_branch_pool.shape[1]
    hd = hgrn_lower_bounds.shape[1]

    f_lo = pd + hd
    win = w_in[0].astype(BF16)
    wg = w_pool_group.astype(BF16)
    layer_weights = (w_branch_pool, w_branch_hgrn, w_mix_out, w_xq, w_xkv, w_xo, w_ffn_in, w_ffn_out)

    as_rows = lambda a: a.reshape(a.shape[0], 1, a.shape[1])
    g_mix, g_mem, g_cross, g_ffn = map(as_rows, (norm_mix, norm_mem, norm_cross, norm_ffn))
    g_hgrn, sc_pool = as_rows(hgrn_norm), as_rows(pool_scale)
    g_final = norm_final.reshape(1, d)

    mall_np, lev_np, nlev = _hgrn_tables(HGRN_CHUNK)
    mall = jnp.asarray(mall_np, BF16)
    lev = jnp.asarray(lev_np)

    t = _tiles(batch * seq, seq, batch * mem_len, hd)
    h = x.reshape(batch * seq, d)
    mem2 = mem.reshape(batch * mem_len, d)
    for l in range(depth):
        zr, zf = _in_proj(h, g_mix, win, l, t["proj_rows"], hd, f_lo)
        casts = [(w, l) for w in layer_weights] + ([(w_in, l + 1)] if l + 1 < depth else [])
        b_out, wbp, wbh, wmo, wxq, wxkv, wxo, wfi, wfo, *nxt = _hgrn(
            zr, zf, hgrn_lower_bounds, g_hgrn, l, batch, seq, mall, lev, nlev, casts)
        win = nxt[0] if nxt else None
        h = _mix(zr, b_out, h, wg, sc_pool, wbp, wbh, wmo, l, batch, seq, t["mix_rows"])
        kv = _norm_mm(mem2, g_mem, wxkv, l, BF16, t["kv_rows"], t["kv_cols"])
        h = _xattn(h, g_cross, wxq, kv, wxo, l, batch, seq, mem_len, t["xattn_rows"])
        h = _ffn(h, g_ffn, wfi, wfo, l, t["ffn_rows"], t["ffn_cols"], g_final if l == depth - 1 else None)
    return h.reshape(batch, seq, d)
```

```python
import functools

import numpy as np
import jax
import jax.numpy as jnp
from jax import lax
from jax.experimental import pallas as pl
from jax.experimental.pallas import tpu as pltpu

F32 = jnp.float32
BF16 = jnp.bfloat16
EPS = 1e-6
LOG2E = 1.4426950408889634

SUBLANES = 8
LANES = 128
BF16_ROW_TILE = 16

POOL_WINDOWS = (2, 4, 8, 16)
POOL_HALO = 16
MIX_SLABS = 2
MIX_COL_CHUNKS = 4
XATTN_SLABS = 2
FFN_COL_CHUNKS = 2
PROJ_NORM_SLABS = 4
HGRN_HEAD = 128
HGRN_CHUNK = 128
X_HEADS = 4

VMEM_LIMIT = 56 * 1024 * 1024


def _params(sem):
    return pltpu.CompilerParams(dimension_semantics=sem, vmem_limit_bytes=VMEM_LIMIT)


def _resident(block_shape, index_map):
    return pl.BlockSpec(block_shape, index_map, pipeline_mode=pl.Buffered(1))


def _rmsnorm(x, gain):
    ms = jnp.mean(x * x, axis=-1, keepdims=True)
    return x * lax.rsqrt(ms + EPS) * gain


def _dot(a, b):
    return jnp.dot(a, b, preferred_element_type=F32)


def _dot_nt(a, b):
    return lax.dot_general(a, b, (((1,), (1,)), ((), ())), preferred_element_type=F32)


def _dot_tn(a, b):
    return lax.dot_general(a, b, (((0,), (0,)), ((), ())), preferred_element_type=F32)


def _sigmoid(x):
    return 1.0 / (1.0 + jnp.exp(-x))


def _norm_mm_kernel(x_ref, g_ref, w_ref, o_ref, u_ref):
    @pl.when(pl.program_id(1) == 0)
    def _():
        u_ref[...] = _rmsnorm(x_ref[...], g_ref[...]).astype(BF16)

    o_ref[...] = _dot(u_ref[...], w_ref[...]).astype(o_ref.dtype)


def _norm_mm(x, gain_all, w, layer, out_dtype, tm, tn):
    m, d = x.shape
    n = w.shape[-1]
    return pl.pallas_call(
        _norm_mm_kernel,
        grid=(m // tm, n // tn),
        in_specs=[
            pl.BlockSpec((tm, d), lambda i, j: (i, 0)),
            pl.BlockSpec((None, 1, d), lambda i, j: (layer, 0, 0)),
            pl.BlockSpec((d, tn), lambda i, j: (0, j)),
        ],
        out_specs=pl.BlockSpec((tm, tn), lambda i, j: (i, j)),
        out_shape=jax.ShapeDtypeStruct((m, n), out_dtype),
        scratch_shapes=[pltpu.VMEM((tm, d), BF16)],
        compiler_params=_params(("parallel", "arbitrary")),
        name="norm_mm",
    )(x, gain_all, w)


def _in_proj_kernel(x_ref, g_ref, w_ref, zr_ref, zf_ref, u_ref):
    j = pl.program_id(1)

    @pl.when(j == 0)
    def _():
        rows = x_ref.shape[0] // PROJ_NORM_SLABS
        for r in range(PROJ_NORM_SLABS):
            rs = slice(r * rows, (r + 1) * rows)
            u = _rmsnorm(x_ref[rs, :], g_ref[...]).astype(BF16)
            u_ref[rs, :] = u
            z = _dot(u, w_ref[...])
            zr_ref[rs, :] = z.astype(BF16)
            zf_ref[rs, :] = z

    @pl.when(j > 0)
    def _():
        z = _dot(u_ref[...], w_ref[...])
        zr_ref[...] = z.astype(BF16)
        zf_ref[...] = z


def _in_proj(x, gain_all, w, layer, tm, tn, f_lo):
    m, d = x.shape
    n = w.shape[-1]
    nb = n // tn
    f_blk = f_lo // tn

    def w_col(j):
        return jnp.where(j == nb - 1, f_blk, j + (j >= f_blk).astype(jnp.int32))

    return pl.pallas_call(
        _in_proj_kernel,
        grid=(m // tm, nb),
        in_specs=[
            pl.BlockSpec((tm, d), lambda i, j: (i, 0)),
            pl.BlockSpec((None, 1, d), lambda i, j: (layer, 0, 0)),
            pl.BlockSpec((d, tn), lambda i, j: (0, w_col(j))),
        ],
        out_specs=[
            pl.BlockSpec((tm, tn), lambda i, j: (i, j)),
            pl.BlockSpec((tm, tn), lambda i, j: (i, 0)),
        ],
        out_shape=[jax.ShapeDtypeStruct((m, n), BF16), jax.ShapeDtypeStruct((m, tn), F32)],
        scratch_shapes=[pltpu.VMEM((tm, d), BF16)],
        compiler_params=_params(("parallel", "arbitrary")),
        name="in_proj",
    )(x, gain_all, w)


def _hgrn_tables(c):
    nlev = int(np.log2(c))
    t = np.arange(c)[:, None]
    j = np.arange(c)[None, :]
    blocks = [(j <= t)]
    for l in range(1, nlev + 1):
        m = 2 ** (l - 1)
        r = (t // (2 * m)) * (2 * m) + m - 1
        right = (t % (2 * m)) >= m
        blocks.append(np.where(right, (j > r) & (j <= t), (j > t) & (j <= r)))
    blocks.append(j > t)
    mall = np.concatenate(blocks, axis=0).astype(np.float32)
    mall = np.concatenate([mall, mall], axis=1)
    x = t ^ j
    lev = np.zeros((c, c), np.int32)
    low = j < t
    lev[low] = np.floor(np.log2(x[low])).astype(np.int32) + 1
    lev[np.arange(c), np.arange(c)] = nlev + 1
    return mall, lev, nlev


def _silu(z):
    return z / (1.0 + jnp.exp(-z))


def _hgrn_kernel(lb_ref, q_ref, f_ref, i_ref, og_ref, gain_ref, mall_ref, lev_ref, *rest,
                 layer, nlev, heads, n_cast):
    cast_src = rest[:n_cast]
    o_ref = rest[n_cast]
    cast_dst = rest[n_cast + 1:2 * n_cast + 1]
    state_ref, lbs_ref, w_scr = rest[2 * n_cast + 1:]
    c = HGRN_CHUNK
    hw = HGRN_HEAD
    pw = 2 * hw

    @pl.when(pl.program_id(1) == 0)
    def _():
        state_ref[...] = jnp.zeros_like(state_ref)
        x = lb_ref[...]
        e = jnp.exp(x - jnp.max(x, axis=0, keepdims=True))
        sm = e / jnp.sum(e, axis=0, keepdims=True)
        lb = jnp.zeros_like(sm[0:1])
        for r in range(1, layer + 1):
            lb = lb + sm[r:r + 1]
        lbs_ref[0:1, :] = jnp.log2(lb)
        lbs_ref[1:2, :] = jnp.log2(1.0 - lb)
        lbs_ref[2:3, :] = 1.0 - lb

    sub = SUBLANES
    nblk = c // sub
    lev = lev_ref[...]
    lev_blk = [lev[i * sub:(i + 1) * sub] for i in range(nblk)]
    row_in_blk = lax.broadcasted_iota(jnp.int32, (sub, hw), 0)

    def is_right(row, m):
        return (row & m) != 0

    def level_operand(q_rows, k_rows, w_l, m):
        if m >= sub:
            qk = [q_rows[i] if is_right(i * sub, m) else k_rows[i] for i in range(nblk)]
        else:
            pick_q = is_right(row_in_blk, m)
            qk = [jnp.where(pick_q, q_rows[i], k_rows[i]) for i in range(nblk)]
        return (jnp.concatenate(qk, axis=0) * w_l).astype(BF16)

    def one_head(q_h, k_h, w_h, v, hidx):
        q_rows = [q_h[i * sub:(i + 1) * sub] for i in range(nblk)]
        k_rows = [k_h[i * sub:(i + 1) * sub] for i in range(nblk)]
        s0 = _dot_nt(q_h.astype(BF16), k_h.astype(BF16))
        s_lv = []
        for l in range(1, nlev + 1):
            x = level_operand(q_rows, k_rows, w_h[l * c:(l + 1) * c], 2 ** (l - 1))
            s_lv.append(_dot_nt(x, x))
        s_rows = [jnp.where(lev_blk[i] == nlev + 1, s0[i * sub:(i + 1) * sub], 0.0) for i in range(nblk)]
        for l in range(1, nlev + 1):
            m = 2 ** (l - 1)
            s_l = s_lv[l - 1]
            for i in range(nblk):
                if m < sub or is_right(i * sub, m):
                    s_rows[i] = jnp.where(lev_blk[i] == l, s_l[i * sub:(i + 1) * sub], s_rows[i])
        s = jnp.concatenate(s_rows, axis=0).astype(BF16)
        state_t = state_ref[hidx]
        o = _dot(s, v) + _dot_nt((q_h * w_h[0:c]).astype(BF16), state_t.astype(BF16))
        k_dec = (k_h * w_h[(nlev + 1) * c:(nlev + 2) * c]).astype(BF16)
        state_ref[hidx] = state_t * w_h[c - 1:c] + _dot_tn(v, k_dec)
        return o * lax.rsqrt(jnp.mean(o * o, axis=1, keepdims=True) + EPS)

    def gates(hp):
        ps = slice(hp * pw, (hp + 1) * pw)
        q = _silu(q_ref[:, ps].astype(F32))
        log2_lb = lbs_ref[0:1, ps]
        log2_1m_lb = lbs_ref[1:2, ps]
        one_m_lb = lbs_ref[2:3, ps]
        z2 = f_ref[:, ps] * LOG2E
        e = jnp.exp2(-jnp.abs(z2))
        ope = 1.0 + e
        rcp = 1.0 / ope
        k = one_m_lb * jnp.where(z2 >= 0, e * rcp, rcp)
        cc = log2_1m_lb + (jnp.minimum(z2, 0.0) - jnp.log2(ope))
        g = jnp.maximum(log2_lb, cc) + jnp.log2(1.0 + jnp.exp2(-jnp.abs(log2_lb - cc)))
        g_hi = g.astype(BF16)
        g_lo = (g - g_hi.astype(F32)).astype(BF16)
        w_scr[hp % 2] = jnp.exp2(_dot(mall_ref[...], jnp.concatenate([g_hi, g_lo], axis=0)))
        return q, k

    nxt = gates(0)
    for hp in range(heads // 2):
        ps = slice(hp * pw, (hp + 1) * pw)
        q, k = nxt
        outs = []
        for j in range(2):
            cols = slice(j * hw, (j + 1) * hw)
            v = i_ref[:, hp * pw + j * hw:hp * pw + (j + 1) * hw]
            outs.append(one_head(q[:, cols], k[:, cols], w_scr.at[hp % 2, :, cols], v, 2 * hp + j))
            if j == 0 and hp + 1 < heads // 2:
                nxt = gates(hp + 1)
        o = jnp.concatenate(outs, axis=1) * gain_ref[:, ps]
        o_ref[:, ps] = (o * _silu(og_ref[:, ps].astype(F32))).astype(BF16)
        for src, dst in list(zip(cast_src, cast_dst))[hp::heads // 2]:
            dst[...] = src[...].astype(BF16)


def _cast_plan(w, layer, steps):
    _, r, cols = w.shape
    csplit = 1
    while r % ((steps // csplit) * BF16_ROW_TILE) != 0:
        csplit *= 2
    assert steps % csplit == 0 and cols % (csplit * LANES) == 0, (w.shape, steps)
    br, bc = r // (steps // csplit), cols // csplit
    in_spec = lambda step: pl.BlockSpec((None, br, bc), lambda b, t: (layer, step(b, t) // csplit, step(b, t) % csplit))
    out_spec = lambda step: pl.BlockSpec((br, bc), lambda b, t: (step(b, t) // csplit, step(b, t) % csplit))
    return in_spec, out_spec, jax.ShapeDtypeStruct((r, cols), BF16)


def _hgrn(zr, zf, lb_raw, gain_all, layer, batch, seq, mall, lev, nlev, casts):
    m = zr.shape[0]
    hd = zf.shape[1]
    depth = lb_raw.shape[0]
    c = HGRN_CHUNK
    nc = seq // c
    heads = hd // HGRN_HEAD
    row = lambda b, t: b * nc + t
    plans = [_cast_plan(w, wl, batch * nc) for w, wl in casts]
    return pl.pallas_call(
        functools.partial(_hgrn_kernel, layer=layer, nlev=nlev, heads=heads, n_cast=len(casts)),
        grid=(batch, nc),
        in_specs=[
            _resident((depth, hd), lambda b, t: (0, 0)),
            pl.BlockSpec((c, hd), lambda b, t: (row(b, t), 1)),
            pl.BlockSpec((c, hd), lambda b, t: (row(b, t), 0)),
            pl.BlockSpec((c, hd), lambda b, t: (row(b, t), 2)),
            pl.BlockSpec((c, hd), lambda b, t: (row(b, t), 3)),
            pl.BlockSpec((None, 1, hd), lambda b, t: (layer, 0, 0)),
            _resident(mall.shape, lambda b, t: (0, 0)),
            _resident(lev.shape, lambda b, t: (0, 0)),
        ] + [p[0](row) for p in plans],
        out_specs=[pl.BlockSpec((c, hd), lambda b, t: (row(b, t), 0))] + [p[1](row) for p in plans],
        out_shape=[jax.ShapeDtypeStruct((m, hd), BF16)] + [p[2] for p in plans],
        scratch_shapes=[pltpu.VMEM((heads, HGRN_HEAD, HGRN_HEAD), F32),
                        pltpu.VMEM((8, hd), F32),
                        pltpu.VMEM((2, mall.shape[0], 2 * HGRN_HEAD), F32)],
        compiler_params=_params(("parallel", "arbitrary")),
        name="hgrn",
    )(lb_raw, zr, zf, zr, zr, gain_all, mall, lev, *[w for w, _ in casts])


def _mix_kernel(zp_ref, halo_ref, ga_ref, gb_ref, bo_ref, h_ref, wg_ref, sc_ref,
                wbp_ref, wbh_ref, wmo_ref, o_ref, *, tm):
    rows = tm // MIX_SLABS
    gd = zp_ref.shape[1] // len(POOL_WINDOWS)
    for part in range(MIX_SLABS):
        rs = slice(part * rows, (part + 1) * rows)
        t0 = pl.program_id(1) * tm + part * rows
        x = zp_ref[rs, :].astype(F32)
        if part == 0:
            halo = jnp.where(t0 > 0, halo_ref[...].astype(F32), 0.0)
        else:
            halo = zp_ref[part * rows - POOL_HALO:part * rows, :].astype(F32)
        ext = jnp.concatenate([halo, x], axis=0)
        pos = lax.broadcasted_iota(jnp.int32, (rows, 1), 0) + t0
        outs = []
        for g, wdw in enumerate(POOL_WINDOWS):
            cols = slice(g * gd, (g + 1) * gd)
            e = ext[:, cols]
            sh = 1
            while sh < wdw:
                e = e + pltpu.roll(e, sh, axis=0)
                sh *= 2
            cnt = jnp.minimum(pos + 1, wdw).astype(F32)
            pooled = (e[POOL_HALO:] / cnt - x[:, cols]).astype(BF16)
            outs.append((_dot(pooled, wg_ref[g]) * sc_ref[:, cols]).astype(BF16))
        a_out = jnp.concatenate(outs, axis=1)
        d = o_ref.shape[1]
        cw = d // MIX_COL_CHUNKS
        chunks = []
        for cc in range(MIX_COL_CHUNKS):
            cs = slice(cc * cw, (cc + 1) * cw)
            part_sum = _sigmoid(ga_ref[rs, cs].astype(F32)) * _dot(a_out, wbp_ref[:, cs])
            part_sum = part_sum + _sigmoid(gb_ref[rs, cs].astype(F32)) * _dot(bo_ref[rs, :], wbh_ref[:, cs])
            chunks.append(part_sum.astype(BF16))
        merged = jnp.concatenate(chunks, axis=1)
        o_ref[rs, :] = h_ref[rs, :] + _dot(merged, wmo_ref[...])


def _mix(zr, b_out, h, wg_all, sc_all, wbp, wbh, wmo, layer, batch, seq, tm):
    m, d = h.shape
    pd = b_out.shape[1]
    ng, gd = wg_all.shape[1], wg_all.shape[2]
    nt = seq // tm
    row = lambda b, t: b * nt + t
    hb = tm // POOL_HALO
    return pl.pallas_call(
        functools.partial(_mix_kernel, tm=tm),
        grid=(batch, nt),
        in_specs=[
            pl.BlockSpec((tm, pd), lambda b, t: (row(b, t), 0)),
            pl.BlockSpec((POOL_HALO, pd), lambda b, t: (jnp.maximum(row(b, t) * hb - 1, 0), 0)),
            pl.BlockSpec((tm, d), lambda b, t: (row(b, t), 2)),
            pl.BlockSpec((tm, d), lambda b, t: (row(b, t), 3)),
            pl.BlockSpec((tm, pd), lambda b, t: (row(b, t), 0)),
            pl.BlockSpec((tm, d), lambda b, t: (row(b, t), 0)),
            _resident((None, ng, gd, gd), lambda b, t: (layer, 0, 0, 0)),
            _resident((None, 1, pd), lambda b, t: (layer, 0, 0)),
            _resident((pd, d), lambda b, t: (0, 0)),
            _resident((pd, d), lambda b, t: (0, 0)),
            _resident((d, d), lambda b, t: (0, 0)),
        ],
        out_specs=pl.BlockSpec((tm, d), lambda b, t: (row(b, t), 0)),
        out_shape=jax.ShapeDtypeStruct((m, d), F32),
        compiler_params=_params(("parallel", "parallel")),
        name="mix",
    )(zr, zr, zr, zr, b_out, h, wg_all, sc_all, wbp, wbh, wmo)


def _xattn_kernel(h_ref, g_ref, wq_ref, kv_ref, wo_ref, o_ref, att_ref):
    h = h_ref[...]
    d = h.shape[1]
    dh = d // X_HEADS
    rows = h.shape[0] // XATTN_SLABS
    q = jnp.concatenate(
        [_dot(_rmsnorm(h[r * rows:(r + 1) * rows], g_ref[...]).astype(BF16), wq_ref[...]).astype(BF16)
         for r in range(XATTN_SLABS)], axis=0)
    for hh in range(X_HEADS):
        cols = slice(hh * dh, (hh + 1) * dh)
        s = _dot_nt(q[:, cols], kv_ref[:, cols]) * (dh ** -0.5)
        p = jnp.exp(s - jnp.max(s, axis=-1, keepdims=True))
        p = p / jnp.sum(p, axis=-1, keepdims=True)
        att_ref[:, cols] = _dot(p.astype(BF16), kv_ref[:, d + hh * dh:d + (hh + 1) * dh]).astype(BF16)
    o_ref[...] = h + _dot(att_ref[...], wo_ref[...])


def _xattn(h, gain_all, wq, kv, wo, layer, batch, seq, mem_len, tm):
    m, d = h.shape
    nt = seq // tm
    row = lambda b, t: b * nt + t
    return pl.pallas_call(
        _xattn_kernel,
        grid=(batch, nt),
        in_specs=[
            pl.BlockSpec((tm, d), lambda b, t: (row(b, t), 0)),
            pl.BlockSpec((None, 1, d), lambda b, t: (layer, 0, 0)),
            _resident((d, d), lambda b, t: (0, 0)),
            pl.BlockSpec((mem_len, 2 * d), lambda b, t: (b, 0)),
            _resident((d, d), lambda b, t: (0, 0)),
        ],
        out_specs=pl.BlockSpec((tm, d), lambda b, t: (row(b, t), 0)),
        out_shape=jax.ShapeDtypeStruct((m, d), F32),
        scratch_shapes=[pltpu.VMEM((tm, d), BF16)],
        compiler_params=_params(("parallel", "parallel")),
        name="xattn",
    )(h, gain_all, wq, kv, wo)


def _ffn_kernel(h_ref, g_ref, wg_ref, wu_ref, wo_ref, *rest, final):
    if final:
        gf_ref, o_ref, u_ref = rest
    else:
        o_ref, u_ref = rest
    f = pl.program_id(1)

    def partial_sum(u):
        tf = wg_ref.shape[1]
        cw = tf // FFN_COL_CHUNKS
        acts = []
        for cc in range(FFN_COL_CHUNKS):
            cs = slice(cc * cw, (cc + 1) * cw)
            gate = _dot(u, wg_ref[:, cs])
            up = _dot(u, wu_ref[:, cs])
            acts.append((gate * _sigmoid(gate) * up).astype(BF16))
        return _dot(jnp.concatenate(acts, axis=1), wo_ref[...])

    @pl.when(f == 0)
    def _():
        h = h_ref[...]
        u = _rmsnorm(h, g_ref[...]).astype(BF16)
        u_ref[...] = u
        o_ref[...] = h + partial_sum(u)

    @pl.when(f > 0)
    def _():
        o_ref[...] += partial_sum(u_ref[...])

    if final:
        @pl.when(f == pl.num_programs(1) - 1)
        def _():
            o_ref[...] = _rmsnorm(o_ref[...], gf_ref[...])


def _ffn(h, gain_all, w_in, w_out, layer, tm, tf, final_gain=None):
    m, d = h.shape
    dff = w_out.shape[0]
    nf = dff // tf
    final = final_gain is not None
    in_specs = [
        pl.BlockSpec((tm, d), lambda i, f: (i, 0)),
        pl.BlockSpec((None, 1, d), lambda i, f: (layer, 0, 0)),
        pl.BlockSpec((d, tf), lambda i, f: (0, f)),
        pl.BlockSpec((d, tf), lambda i, f: (0, nf + f)),
        pl.BlockSpec((tf, d), lambda i, f: (f, 0)),
    ]
    args = [h, gain_all, w_in, w_in, w_out]
    if final:
        in_specs.append(pl.BlockSpec((1, d), lambda i, f: (0, 0)))
        args.append(final_gain)
    return pl.pallas_call(
        functools.partial(_ffn_kernel, final=final),
        grid=(m // tm, nf),
        in_specs=in_specs,
        out_specs=pl.BlockSpec((tm, d), lambda i, f: (i, 0)),
        out_shape=jax.ShapeDtypeStruct((m, d), F32),
        scratch_shapes=[pltpu.VMEM((tm, d), BF16)],
        compiler_params=_params(("parallel", "arbitrary")),
        name="ffn",
    )(*args)


def _tiles(tokens, seq, mem_tokens, hd):
    return {
        "proj_rows": min(1024, tokens),
        "mix_rows": min(512, seq),
        "kv_rows": min(1024, mem_tokens),
        "kv_cols": hd,
        "xattn_rows": min(512, seq),
        "ffn_rows": min(1024, tokens),
        "ffn_cols": 512,
    }


def kernel(x, mem, w_in, w_pool_group, pool_scale, hgrn_lower_bounds, hgrn_norm, w_branch_pool,
           w_branch_hgrn, w_mix_out, norm_mix, norm_mem, norm_cross, w_xq, w_xkv, w_xo, norm_ffn,
           w_ffn_in, w_ffn_out, norm_final):
    batch, seq, d = x.shape
    mem_len = mem.shape[1]
    depth = w_in.shape[0]
    pd = w_branch_pool.shape[1]
    hd = hgrn_lower_bounds.shape[1]

    f_lo = pd + hd
    win = w_in[0].astype(BF16)
    wg = w_pool_group.astype(BF16)
    layer_weights = (w_branch_pool, w_branch_hgrn, w_mix_out, w_xq, w_xkv, w_xo, w_ffn_in, w_ffn_out)

    as_rows = lambda a: a.reshape(a.shape[0], 1, a.shape[1])
    g_mix, g_mem, g_cross, g_ffn = map(as_rows, (norm_mix, norm_mem, norm_cross, norm_ffn))
    g_hgrn, sc_pool = as_rows(hgrn_norm), as_rows(pool_scale)
    g_final = norm_final.reshape(1, d)

    mall_np, lev_np, nlev = _hgrn_tables(HGRN_CHUNK)
    mall = jnp.asarray(mall_np, BF16)
    lev = jnp.asarray(lev_np)

    t = _tiles(batch * seq, seq, batch * mem_len, hd)
    h = x.reshape(batch * seq, d)
    mem2 = mem.reshape(batch * mem_len, d)
    for l in range(depth):
        zr, zf = _in_proj(h, g_mix, win, l, t["proj_rows"], hd, f_lo)
        casts = [(w, l) for w in layer_weights] + ([(w_in, l + 1)] if l + 1 < depth else [])
        b_out, wbp, wbh, wmo, wxq, wxkv, wxo, wfi, wfo, *nxt = _hgrn(
            zr, zf, hgrn_lower_bounds, g_hgrn, l, batch, seq, mall, lev, nlev, casts)
        win = nxt[0] if nxt else None
        h = _mix(zr, b_out, h, wg, sc_pool, wbp, wbh, wmo, l, batch, seq, t["mix_rows"])
        kv = _norm_mm(mem2, g_mem, wxkv, l, BF16, t["kv_rows"], t["kv_cols"])
        h = _xattn(h, g_cross, wxq, kv, wxo, l, batch, seq, mem_len, t["xattn_rows"])
        h = _ffn(h, g_ffn, wfi, wfo, l, t["ffn_rows"], t["ffn_cols"], g_final if l == depth - 1 else None)
    return h.reshape(batch, seq, d)
```

```python
import functools

import numpy as np
import jax
import jax.numpy as jnp
from jax import lax
from jax.experimental import pallas as pl
from jax.experimental.pallas import tpu as pltpu

F32 = jnp.float32
BF16 = jnp.bfloat16
EPS = 1e-6
LOG2E = 1.4426950408889634

SUBLANES = 8
LANES = 128
BF16_ROW_TILE = 16

POOL_WINDOWS = (2, 4, 8, 16)
POOL_HALO = 16
MIX_SLABS = 2
MIX_COL_CHUNKS = 4
XATTN_SLABS = 2
FFN_COL_CHUNKS = 2
PROJ_NORM_SLABS = 4
HGRN_HEAD = 128
HGRN_CHUNK = 128
X_HEADS = 4

VMEM_LIMIT = 56 * 1024 * 1024


def _params(sem):
    return pltpu.CompilerParams(dimension_semantics=sem, vmem_limit_bytes=VMEM_LIMIT)


def _resident(block_shape, index_map):
    return pl.BlockSpec(block_shape, index_map, pipeline_mode=pl.Buffered(1))


def _rmsnorm(x, gain):
    ms = jnp.mean(x * x, axis=-1, keepdims=True)
    return x * lax.rsqrt(ms + EPS) * gain


def _dot(a, b):
    return jnp.dot(a, b, preferred_element_type=F32)


def _dot_nt(a, b):
    return lax.dot_general(a, b, (((1,), (1,)), ((), ())), preferred_element_type=F32)


def _dot_tn(a, b):
    return lax.dot_general(a, b, (((0,), (0,)), ((), ())), preferred_element_type=F32)


def _sigmoid(x):
    return 1.0 / (1.0 + jnp.exp(-x))


def _norm_mm_kernel(x_ref, g_ref, w_ref, o_ref, u_ref):
    @pl.when(pl.program_id(1) == 0)
    def _():
        u_ref[...] = _rmsnorm(x_ref[...], g_ref[...]).astype(BF16)

    o_ref[...] = _dot(u_ref[...], w_ref[...]).astype(o_ref.dtype)


def _norm_mm(x, gain_all, w, layer, out_dtype, tm, tn):
    m, d = x.shape
    n = w.shape[-1]
    return pl.pallas_call(
        _norm_mm_kernel,
        grid=(m // tm, n // tn),
        in_specs=[
            pl.BlockSpec((tm, d), lambda i, j: (i, 0)),
            pl.BlockSpec((None, 1, d), lambda i, j: (layer, 0, 0)),
            pl.BlockSpec((d, tn), lambda i, j: (0, j)),
        ],
        out_specs=pl.BlockSpec((tm, tn), lambda i, j: (i, j)),
        out_shape=jax.ShapeDtypeStruct((m, n), out_dtype),
        scratch_shapes=[pltpu.VMEM((tm, d), BF16)],
        compiler_params=_params(("parallel", "arbitrary")),
        name="norm_mm",
    )(x, gain_all, w)


def _in_proj_kernel(x_ref, g_ref, w_ref, zr_ref, zf_ref, u_ref):
    j = pl.program_id(1)

    @pl.when(j == 0)
    def _():
        rows = x_ref.shape[0] // PROJ_NORM_SLABS
        for r in range(PROJ_NORM_SLABS):
            rs = slice(r * rows, (r + 1) * rows)
            u = _rmsnorm(x_ref[rs, :], g_ref[...]).astype(BF16)
            u_ref[rs, :] = u
            z = _dot(u, w_ref[...])
            zr_ref[rs, :] = z.astype(BF16)
            zf_ref[rs, :] = z

    @pl.when(j > 0)
    def _():
        z = _dot(u_ref[...], w_ref[...])
        zr_ref[...] = z.astype(BF16)
        zf_ref[...] = z


def _in_proj(x, gain_all, w, layer, tm, tn, f_lo):
    m, d = x.shape
    n = w.shape[-1]
    nb = n // tn
    f_blk = f_lo // tn

    def w_col(j):
        return jnp.where(j == nb - 1, f_blk, j + (j >= f_blk).astype(jnp.int32))

    return pl.pallas_call(
        _in_proj_kernel,
        grid=(m // tm, nb),
        in_specs=[
            pl.BlockSpec((tm, d), lambda i, j: (i, 0)),
            pl.BlockSpec((None, 1, d), lambda i, j: (layer, 0, 0)),
            pl.BlockSpec((d, tn), lambda i, j: (0, w_col(j))),
        ],
        out_specs=[
            pl.BlockSpec((tm, tn), lambda i, j: (i, j)),
            pl.BlockSpec((tm, tn), lambda i, j: (i, 0)),
        ],
        out_shape=[jax.ShapeDtypeStruct((m, n), BF16), jax.ShapeDtypeStruct((m, tn), F32)],
        scratch_shapes=[pltpu.VMEM((tm, d), BF16)],
        compiler_params=_params(("parallel", "arbitrary")),
        name="in_proj",
    )(x, gain_all, w)


def _hgrn_tables(c):
    nlev = int(np.log2(c))
    t = np.arange(c)[:, None]
    j = np.arange(c)[None, :]
    blocks = [(j <= t)]
    for l in range(1, nlev + 1):
        m = 2 ** (l - 1)
        r = (t // (2 * m)) * (2 * m) + m - 1
        right = (t % (2 * m)) >= m
        blocks.append(np.where(right, (j > r) & (j <= t), (j > t) & (j <= r)))
    blocks.append(j > t)
    mall = np.concatenate(blocks, axis=0).astype(np.float32)
    mall = np.concatenate([mall, mall], axis=1)
    x = t ^ j
    lev = np.zeros((c, c), np.int32)
    low = j < t
    lev[low] = np.floor(np.log2(x[low])).astype(np.int32) + 1
    lev[np.arange(c), np.arange(c)] = nlev + 1
    return mall, lev, nlev


def _silu(z):
    return z / (1.0 + jnp.exp(-z))


def _hgrn_kernel(lb_ref, q_ref, f_ref, i_ref, og_ref, gain_ref, mall_ref, lev_ref, *rest,
                 layer, nlev, heads, n_cast):
    cast_src = rest[:n_cast]
    o_ref = rest[n_cast]
    cast_dst = rest[n_cast + 1:2 * n_cast + 1]
    state_ref, lbs_ref, w_scr = rest[2 * n_cast + 1:]
    c = HGRN_CHUNK
    hw = HGRN_HEAD
    pw = 2 * hw

    @pl.when(pl.program_id(1) == 0)
    def _():
        state_ref[...] = jnp.zeros_like(state_ref)
        x = lb_ref[...]
        e = jnp.exp(x - jnp.max(x, axis=0, keepdims=True))
        sm = e / jnp.sum(e, axis=0, keepdims=True)
        lb = jnp.zeros_like(sm[0:1])
        for r in range(1, layer + 1):
            lb = lb + sm[r:r + 1]
        lbs_ref[0:1, :] = jnp.log2(lb)
        lbs_ref[1:2, :] = jnp.log2(1.0 - lb)
        lbs_ref[2:3, :] = 1.0 - lb

    sub = SUBLANES
    nblk = c // sub
    lev = lev_ref[...]
    lev_blk = [lev[i * sub:(i + 1) * sub] for i in range(nblk)]
    row_in_blk = lax.broadcasted_iota(jnp.int32, (sub, hw), 0)

    def is_right(row, m):
        return (row & m) != 0

    def level_operand(q_rows, k_rows, w_l, m):
        if m >= sub:
            qk = [q_rows[i] if is_right(i * sub, m) else k_rows[i] for i in range(nblk)]
        else:
            pick_q = is_right(row_in_blk, m)
            qk = [jnp.where(pick_q, q_rows[i], k_rows[i]) for i in range(nblk)]
        return (jnp.concatenate(qk, axis=0) * w_l).astype(BF16)

    def one_head(q_h, k_h, w_h, v, hidx):
        q_rows = [q_h[i * sub:(i + 1) * sub] for i in range(nblk)]
        k_rows = [k_h[i * sub:(i + 1) * sub] for i in range(nblk)]
        s0 = _dot_nt(q_h.astype(BF16), k_h.astype(BF16))
        s_lv = []
        for l in range(1, nlev + 1):
            x = level_operand(q_rows, k_rows, w_h[l * c:(l + 1) * c], 2 ** (l - 1))
            s_lv.append(_dot_nt(x, x))
        s_rows = [jnp.where(lev_blk[i] == nlev + 1, s0[i * sub:(i + 1) * sub], 0.0) for i in range(nblk)]
        for l in range(1, nlev + 1):
            m = 2 ** (l - 1)
            s_l = s_lv[l - 1]
            for i in range(nblk):
                if m < sub or is_right(i * sub, m):
                    s_rows[i] = jnp.where(lev_blk[i] == l, s_l[i * sub:(i + 1) * sub], s_rows[i])
        s = jnp.concatenate(s_rows, axis=0).astype(BF16)
        state_t = state_ref[hidx]
        o = _dot(s, v) + _dot_nt((q_h * w_h[0:c]).astype(BF16), state_t.astype(BF16))
        k_dec = (k_h * w_h[(nlev + 1) * c:(nlev + 2) * c]).astype(BF16)
        state_ref[hidx] = state_t * w_h[c - 1:c] + _dot_tn(v, k_dec)
        return o * lax.rsqrt(jnp.mean(o * o, axis=1, keepdims=True) + EPS)

    def gates(hp):
        ps = slice(hp * pw, (hp + 1) * pw)
        q = _silu(q_ref[:, ps].astype(F32))
        log2_lb = lbs_ref[0:1, ps]
        log2_1m_lb = lbs_ref[1:2, ps]
        one_m_lb = lbs_ref[2:3, ps]
        z2 = f_ref[:, ps] * LOG2E
        e = jnp.exp2(-jnp.abs(z2))
        ope = 1.0 + e
        rcp = 1.0 / ope
        k = one_m_lb * jnp.where(z2 >= 0, e * rcp, rcp)
        cc = log2_1m_lb + (jnp.minimum(z2, 0.0) - jnp.log2(ope))
        g = jnp.maximum(log2_lb, cc) + jnp.log2(1.0 + jnp.exp2(-jnp.abs(log2_lb - cc)))
        g_hi = g.astype(BF16)
        g_lo = (g - g_hi.astype(F32)).astype(BF16)
        w_scr[hp % 2] = jnp.exp2(_dot(mall_ref[...], jnp.concatenate([g_hi, g_lo], axis=0)))
        return q, k

    nxt = gates(0)
    for hp in range(heads // 2):
        ps = slice(hp * pw, (hp + 1) * pw)
        q, k = nxt
        outs = []
        for j in range(2):
            cols = slice(j * hw, (j + 1) * hw)
            v = i_ref[:, hp * pw + j * hw:hp * pw + (j + 1) * hw]
            outs.append(one_head(q[:, cols], k[:, cols], w_scr.at[hp % 2, :, cols], v, 2 * hp + j))
            if j == 0 and hp + 1 < heads // 2:
                nxt = gates(hp + 1)
        o = jnp.concatenate(outs, axis=1) * gain_ref[:, ps]
        o_ref[:, ps] = (o * _silu(og_ref[:, ps].astype(F32))).astype(BF16)
        for src, dst in list(zip(cast_src, cast_dst))[hp::heads // 2]:
            dst[...] = src[...].astype(BF16)


def _cast_plan(w, layer, steps):
    _, r, cols = w.shape
    csplit = 1
    while r % ((steps // csplit) * BF16_ROW_TILE) != 0:
        csplit *= 2
    assert steps % csplit == 0 and cols % (csplit * LANES) == 0, (w.shape, steps)
    br, bc = r // (steps // csplit), cols // csplit
    in_spec = lambda step: pl.BlockSpec((None, br, bc), lambda b, t: (layer, step(b, t) // csplit, step(b, t) % csplit))
    out_spec = lambda step: pl.BlockSpec((br, bc), lambda b, t: (step(b, t) // csplit, step(b, t) % csplit))
    return in_spec, out_spec, jax.ShapeDtypeStruct((r, cols), BF16)


def _hgrn(zr, zf, lb_raw, gain_all, layer, batch, seq, mall, lev, nlev, casts):
    m = zr.shape[0]
    hd = zf.shape[1]
    depth = lb_raw.shape[0]
    c = HGRN_CHUNK
    nc = seq // c
    heads = hd // HGRN_HEAD
    row = lambda b, t: b * nc + t
    plans = [_cast_plan(w, wl, batch * nc) for w, wl in casts]
    return pl.pallas_call(
        functools.partial(_hgrn_kernel, layer=layer, nlev=nlev, heads=heads, n_cast=len(casts)),
        grid=(batch, nc),
        in_specs=[
            _resident((depth, hd), lambda b, t: (0, 0)),
            pl.BlockSpec((c, hd), lambda b, t: (row(b, t), 1)),
            pl.BlockSpec((c, hd), lambda b, t: (row(b, t), 0)),
            pl.BlockSpec((c, hd), lambda b, t: (row(b, t), 2)),
            pl.BlockSpec((c, hd), lambda b, t: (row(b, t), 3)),
            pl.BlockSpec((None, 1, hd), lambda b, t: (layer, 0, 0)),
            _resident(mall.shape, lambda b, t: (0, 0)),
            _resident(lev.shape, lambda b, t: (0, 0)),
        ] + [p[0](row) for p in plans],
        out_specs=[pl.BlockSpec((c, hd), lambda b, t: (row(b, t), 0))] + [p[1](row) for p in plans],
        out_shape=[jax.ShapeDtypeStruct((m, hd), BF16)] + [p[2] for p in plans],
        scratch_shapes=[pltpu.VMEM((heads, HGRN_HEAD, HGRN_HEAD), F32),
                        pltpu.VMEM((8, hd), F32),
                        pltpu.VMEM((2, mall.shape[0], 2 * HGRN_HEAD), F32)],
        compiler_params=_params(("parallel", "arbitrary")),
        name="hgrn",
    )(lb_raw, zr, zf, zr, zr, gain_all, mall, lev, *[w for w, _ in casts])


def _mix_kernel(zp_ref, halo_ref, ga_ref, gb_ref, bo_ref, h_ref, wg_ref, sc_ref,
                wbp_ref, wbh_ref, wmo_ref, o_ref, *, tm):
    rows = tm // MIX_SLABS
    gd = zp_ref.shape[1] // len(POOL_WINDOWS)
    for part in range(MIX_SLABS):
        rs = slice(part * rows, (part + 1) * rows)
        t0 = pl.program_id(1) * tm + part * rows
        x = zp_ref[rs, :].astype(F32)
        if part == 0:
            halo = jnp.where(t0 > 0, halo_ref[...].astype(F32), 0.0)
        else:
            halo = zp_ref[part * rows - POOL_HALO:part * rows, :].astype(F32)
        ext = jnp.concatenate([halo, x], axis=0)
        pos = lax.broadcasted_iota(jnp.int32, (rows, 1), 0) + t0
        outs = []
        for g, wdw in enumerate(POOL_WINDOWS):
            cols = slice(g * gd, (g + 1) * gd)
            e = ext[:, cols]
            sh = 1
            while sh < wdw:
                e = e + pltpu.roll(e, sh, axis=0)
                sh *= 2
            cnt = jnp.minimum(pos + 1, wdw).astype(F32)
            pooled = (e[POOL_HALO:] / cnt - x[:, cols]).astype(BF16)
            outs.append((_dot(pooled, wg_ref[g]) * sc_ref[:, cols]).astype(BF16))
        a_out = jnp.concatenate(outs, axis=1)
        d = o_ref.shape[1]
        cw = d // MIX_COL_CHUNKS
        chunks = []
        for cc in range(MIX_COL_CHUNKS):
            cs = slice(cc * cw, (cc + 1) * cw)
            part_sum = _sigmoid(ga_ref[rs, cs].astype(F32)) * _dot(a_out, wbp_ref[:, cs])
            part_sum = part_sum + _sigmoid(gb_ref[rs, cs].astype(F32)) * _dot(bo_ref[rs, :], wbh_ref[:, cs])
            chunks.append(part_sum.astype(BF16))
        merged = jnp.concatenate(chunks, axis=1)
        o_ref[rs, :] = h_ref[rs, :] + _dot(merged, wmo_ref[...])


def _mix(zr, b_out, h, wg_all, sc_all, wbp, wbh, wmo, layer, batch, seq, tm):
    m, d = h.shape
    pd = b_out.shape[1]
    ng, gd = wg_all.shape[1], wg_all.shape[2]
    nt = seq // tm
    row = lambda b, t: b * nt + t
    hb = tm // POOL_HALO
    return pl.pallas_call(
        functools.partial(_mix_kernel, tm=tm),
        grid=(batch, nt),
        in_specs=[
            pl.BlockSpec((tm, pd), lambda b, t: (row(b, t), 0)),
            pl.BlockSpec((POOL_HALO, pd), lambda b, t: (jnp.maximum(row(b, t) * hb - 1, 0), 0)),
            pl.BlockSpec((tm, d), lambda b, t: (row(b, t), 2)),
            pl.BlockSpec((tm, d), lambda b, t: (row(b, t), 3)),
            pl.BlockSpec((tm, pd), lambda b, t: (row(b, t), 0)),
            pl.BlockSpec((tm, d), lambda b, t: (row(b, t), 0)),
            _resident((None, ng, gd, gd), lambda b, t: (layer, 0, 0, 0)),
            _resident((None, 1, pd), lambda b, t: (layer, 0, 0)),
            _resident((pd, d), lambda b, t: (0, 0)),
            _resident((pd, d), lambda b, t: (0, 0)),
            _resident((d, d), lambda b, t: (0, 0)),
        ],
        out_specs=pl.BlockSpec((tm, d), lambda b, t: (row(b, t), 0)),
        out_shape=jax.ShapeDtypeStruct((m, d), F32),
        compiler_params=_params(("parallel", "parallel")),
        name="mix",
    )(zr, zr, zr, zr, b_out, h, wg_all, sc_all, wbp, wbh, wmo)


def _xattn_kernel(h_ref, g_ref, wq_ref, kv_ref, wo_hbm, o_ref, att_ref, wo_ref, wo_sem):
    first = (pl.program_id(0) == 0) & (pl.program_id(1) == 0)
    wo_copy = pltpu.make_async_copy(wo_hbm, wo_ref, wo_sem)

    @pl.when(first)
    def _():
        wo_copy.start()

    h = h_ref[...]
    d = h.shape[1]
    dh = d // X_HEADS
    rows = h.shape[0] // XATTN_SLABS
    q = jnp.concatenate(
        [_dot(_rmsnorm(h[r * rows:(r + 1) * rows], g_ref[...]).astype(BF16), wq_ref[...]).astype(BF16)
         for r in range(XATTN_SLABS)], axis=0)
    for hh in range(X_HEADS):
        cols = slice(hh * dh, (hh + 1) * dh)
        s = _dot_nt(q[:, cols], kv_ref[:, cols]) * (dh ** -0.5)
        p = jnp.exp(s - jnp.max(s, axis=-1, keepdims=True))
        p = p / jnp.sum(p, axis=-1, keepdims=True)
        att_ref[:, cols] = _dot(p.astype(BF16), kv_ref[:, d + hh * dh:d + (hh + 1) * dh]).astype(BF16)

    @pl.when(first)
    def _():
        wo_copy.wait()

    o_ref[...] = h + _dot(att_ref[...], wo_ref[...])


def _xattn(h, gain_all, wq, kv, wo, layer, batch, seq, mem_len, tm):
    m, d = h.shape
    nt = seq // tm
    row = lambda b, t: b * nt + t
    return pl.pallas_call(
        _xattn_kernel,
        grid=(batch, nt),
        in_specs=[
            pl.BlockSpec((tm, d), lambda b, t: (row(b, t), 0)),
            pl.BlockSpec((None, 1, d), lambda b, t: (layer, 0, 0)),
            _resident((d, d), lambda b, t: (0, 0)),
            pl.BlockSpec((mem_len, 2 * d), lambda b, t: (b, 0)),
            pl.BlockSpec(memory_space=pl.ANY),
        ],
        out_specs=pl.BlockSpec((tm, d), lambda b, t: (row(b, t), 0)),
        out_shape=jax.ShapeDtypeStruct((m, d), F32),
        scratch_shapes=[pltpu.VMEM((tm, d), BF16), pltpu.VMEM((d, d), BF16), pltpu.SemaphoreType.DMA(())],
        compiler_params=_params(("arbitrary", "arbitrary")),
        name="xattn",
    )(h, gain_all, wq, kv, wo)


def _ffn_kernel(h_ref, g_ref, wg_ref, wu_ref, wo_ref, *rest, final):
    if final:
        gf_ref, o_ref, u_ref = rest
    else:
        o_ref, u_ref = rest
    f = pl.program_id(1)

    def partial_sum(u):
        tf = wg_ref.shape[1]
        cw = tf // FFN_COL_CHUNKS
        acts = []
        for cc in range(FFN_COL_CHUNKS):
            cs = slice(cc * cw, (cc + 1) * cw)
            gate = _dot(u, wg_ref[:, cs])
            up = _dot(u, wu_ref[:, cs])
            acts.append((gate * _sigmoid(gate) * up).astype(BF16))
        return _dot(jnp.concatenate(acts, axis=1), wo_ref[...])

    @pl.when(f == 0)
    def _():
        h = h_ref[...]
        u = _rmsnorm(h, g_ref[...]).astype(BF16)
        u_ref[...] = u
        o_ref[...] = h + partial_sum(u)

    @pl.when(f > 0)
    def _():
        o_ref[...] += partial_sum(u_ref[...])

    if final:
        @pl.when(f == pl.num_programs(1) - 1)
        def _():
            o_ref[...] = _rmsnorm(o_ref[...], gf_ref[...])


def _ffn(h, gain_all, w_in, w_out, layer, tm, tf, final_gain=None):
    m, d = h.shape
    dff = w_out.shape[0]
    nf = dff // tf
    final = final_gain is not None
    in_specs = [
        pl.BlockSpec((tm, d), lambda i, f: (i, 0)),
        pl.BlockSpec((None, 1, d), lambda i, f: (layer, 0, 0)),
        pl.BlockSpec((d, tf), lambda i, f: (0, f)),
        pl.BlockSpec((d, tf), lambda i, f: (0, nf + f)),
        pl.BlockSpec((tf, d), lambda i, f: (f, 0)),
    ]
    args = [h, gain_all, w_in, w_in, w_out]
    if final:
        in_specs.append(pl.BlockSpec((1, d), lambda i, f: (0, 0)))
        args.append(final_gain)
    return pl.pallas_call(
        functools.partial(_ffn_kernel, final=final),
        grid=(m // tm, nf),
        in_specs=in_specs,
        out_specs=pl.BlockSpec((tm, d), lambda i, f: (i, 0)),
        out_shape=jax.ShapeDtypeStruct((m, d), F32),
        scratch_shapes=[pltpu.VMEM((tm, d), BF16)],
        compiler_params=_params(("parallel", "arbitrary")),
        name="ffn",
    )(*args)


def _tiles(tokens, seq, mem_tokens, hd):
    return {
        "proj_rows": min(1024, tokens),
        "mix_rows": min(512, seq),
        "kv_rows": min(1024, mem_tokens),
        "kv_cols": hd,
        "xattn_rows": min(512, seq),
        "ffn_rows": min(1024, tokens),
        "ffn_cols": 512,
    }


def kernel(x, mem, w_in, w_pool_group, pool_scale, hgrn_lower_bounds, hgrn_norm, w_branch_pool,
           w_branch_hgrn, w_mix_out, norm_mix, norm_mem, norm_cross, w_xq, w_xkv, w_xo, norm_ffn,
           w_ffn_in, w_ffn_out, norm_final):
    batch, seq, d = x.shape
    mem_len = mem.shape[1]
    depth = w_in.shape[0]
    pd = w_branch_pool.shape[1]
    hd = hgrn_lower_bounds.shape[1]

    f_lo = pd + hd
    win = w_in[0].astype(BF16)
    wg = w_pool_group.astype(BF16)
    layer_weights = (w_branch_pool, w_branch_hgrn, w_mix_out, w_xq, w_xkv, w_xo, w_ffn_in, w_ffn_out)

    as_rows = lambda a: a.reshape(a.shape[0], 1, a.shape[1])
    g_mix, g_mem, g_cross, g_ffn = map(as_rows, (norm_mix, norm_mem, norm_cross, norm_ffn))
    g_hgrn, sc_pool = as_rows(hgrn_norm), as_rows(pool_scale)
    g_final = norm_final.reshape(1, d)

    mall_np, lev_np, nlev = _hgrn_tables(HGRN_CHUNK)
    mall = jnp.asarray(mall_np, BF16)
    lev = jnp.asarray(lev_np)

    t = _tiles(batch * seq, seq, batch * mem_len, hd)
    h = x.reshape(batch * seq, d)
    mem2 = mem.reshape(batch * mem_len, d)
    for l in range(depth):
        zr, zf = _in_proj(h, g_mix, win, l, t["proj_rows"], hd, f_lo)
        casts = [(w, l) for w in layer_weights] + ([(w_in, l + 1)] if l + 1 < depth else [])
        b_out, wbp, wbh, wmo, wxq, wxkv, wxo, wfi, wfo, *nxt = _hgrn(
            zr, zf, hgrn_lower_bounds, g_hgrn, l, batch, seq, mall, lev, nlev, casts)
        win = nxt[0] if nxt else None
        h = _mix(zr, b_out, h, wg, sc_pool, wbp, wbh, wmo, l, batch, seq, t["mix_rows"])
        kv = _norm_mm(mem2, g_mem, wxkv, l, BF16, t["kv_rows"], t["kv_cols"])
        h = _xattn(h, g_cross, wxq, kv, wxo, l, batch, seq, mem_len, t["xattn_rows"])
        h = _ffn(h, g_ffn, wfi, wfo, l, t["ffn_rows"], t["ffn_cols"], g_final if l == depth - 1 else None)
    return h.reshape(batch, seq, d)
```
